```python
import math
import jax, jax.numpy as jnp
from jax import lax
import numpy as np

D_MODEL = 1024
BATCH = 8
SEQ = 4096
DEPTH = 2

D_HEAD = 64
GRID_W = 64
Q_BLOCK = 128
RMS_EPS = 1e-6
H_A = D_MODEL // (2 * D_HEAD)
WIN_R = 8
WIN_C = 16
H_B = D_MODEL // (2 * D_HEAD)
H_B_KV = H_B // 4
AXIAL_THETA = 10000.0
H_C = D_MODEL // (2 * D_HEAD)
Q_LORA = D_MODEL // 4
KV_LORA = D_MODEL // 8
C_NOPE = 64
C_ROPE = 32
C_V = 64
MLA_THETA = 10000.0
H_D = D_MODEL // (4 * D_HEAD)
D_V = 2 * D_HEAD
ROPE_THETA = 500000.0
ROT_DIM = D_HEAD // 4
GATE_E = (H_A + H_B) * D_HEAD
SPLIT_E = [H_A * D_HEAD, H_A * D_HEAD, H_A * D_HEAD,
           H_B * D_HEAD, H_B_KV * D_HEAD, H_B_KV * D_HEAD, GATE_E]
GATE_O = H_C * C_V + H_D * D_V
SPLIT_O = [Q_LORA, KV_LORA, C_ROPE, 2 * H_D * D_HEAD, 2 * H_D * D_HEAD,
           H_D * D_V, GATE_O]
IN_E = sum(SPLIT_E)
IN_O = sum(SPLIT_O)
N_EVEN = (DEPTH + 1) // 2
N_ODD = DEPTH // 2

kernel_name = "hybrid_natten_gqa_mla_diff_encoder"


def _split(x, sizes):
    idx = [int(v) for v in np.cumsum(sizes)[:-1]]
    return jnp.split(x, idx, axis=-1)


def rmsnorm(x, g):
    xf = x.astype(jnp.float32)
    y = xf * lax.rsqrt(jnp.mean(xf * xf, axis=-1, keepdims=True) + RMS_EPS)
    return (y * g.astype(jnp.float32)).astype(x.dtype)


def rope_angles(pos, dim, theta):
    inv = jnp.power(theta, -jnp.arange(0, dim, 2, dtype=jnp.float32) / dim)
    return pos[:, None] * inv[None, :]


def rotate(x, ang):
    shp = (ang.shape[0],) + (1,) * (x.ndim - 3) + (ang.shape[1],)
    c = jnp.cos(ang).reshape(shp).astype(x.dtype)
    s = jnp.sin(ang).reshape(shp).astype(x.dtype)
    x1, x2 = jnp.split(x, 2, axis=-1)
    return jnp.concatenate([x1 * c - x2 * s, x2 * c + x1 * s], axis=-1)


def attend_blocks(q, k, v, scale):
    B, S, Hq, dq = q.shape
    Hk, dv = k.shape[2], v.shape[3]
    G = Hq // Hk
    nb = S // Q_BLOCK
    qb = q.reshape(B, nb, Q_BLOCK, Hk, G, dq).transpose(1, 0, 2, 3, 4, 5)

    def one(qi):
        s = jnp.einsum('bqhgd,bkhd->bhgqk', qi, k).astype(jnp.float32) * scale
        p = jax.nn.softmax(s, axis=-1).astype(v.dtype)
        return jnp.einsum('bhgqk,bkhd->bqhgd', p, v)

    o = lax.map(one, qb)
    return o.transpose(1, 0, 2, 3, 4, 5).reshape(B, S, Hq, dv)


def diff_attend_blocks(q1, q2, k1, k2, v, lam, scale):
    B, S, H, d = q1.shape
    nb = S // Q_BLOCK
    q1b = q1.reshape(B, nb, Q_BLOCK, H, d).transpose(1, 0, 2, 3, 4)
    q2b = q2.reshape(B, nb, Q_BLOCK, H, d).transpose(1, 0, 2, 3, 4)

    def one(args):
        a, b = args
        s1 = jnp.einsum('bqhd,bkhd->bhqk', a, k1).astype(jnp.float32) * scale
        s2 = jnp.einsum('bqhd,bkhd->bhqk', b, k2).astype(jnp.float32) * scale
        p = jax.nn.softmax(s1, axis=-1) - lam * jax.nn.softmax(s2, axis=-1)
        return jnp.einsum('bhqk,bkhd->bqhd', p.astype(v.dtype), v)

    o = lax.map(one, (q1b, q2b))
    return o.transpose(1, 0, 2, 3, 4).reshape(B, S, H, v.shape[-1])


def neighbourhood_attention(q, k, v, rpb):
    B, S, H, d = q.shape
    rows = S // GRID_W
    wr = min(WIN_R, rows)
    wc = WIN_C
    qg = q.reshape(B, rows, GRID_W, H, d)
    kg = k.reshape(B, rows, GRID_W, H, d)
    vg = v.reshape(B, rows, GRID_W, H, d)
    col = jnp.arange(GRID_W)
    c0 = jnp.clip(col - wc // 2, 0, GRID_W - wc)
    cidx = c0[:, None] + jnp.arange(wc)[None, :]
    cb = (cidx - col[:, None] + (WIN_C - 1))[:, None, :]
    scale = d ** -0.5

    def one(r):
        r0 = jnp.clip(r - wr // 2, 0, rows - wr)
        kr = lax.dynamic_slice_in_dim(kg, r0, wr, axis=1)
        vr = lax.dynamic_slice_in_dim(vg, r0, wr, axis=1)
        kw = kr[:, :, cidx]
        vw = vr[:, :, cidx]
        qr = lax.dynamic_index_in_dim(qg, r, axis=1, keepdims=False)
        s = jnp.einsum('bqhd,biqjhd->bhqij', qr, kw).astype(jnp.float32) * scale
        rb = (r0 + jnp.arange(wr) - r + (WIN_R - 1))[None, :, None]
        bias = rpb[:, rb, cb].astype(jnp.float32)
        s = (s + bias[None]).reshape(B, H, GRID_W, wr * wc)
        p = jax.nn.softmax(s, axis=-1).reshape(B, H, GRID_W, wr, wc).astype(v.dtype)
        return jnp.einsum('bhqij,biqjhd->bqhd', p, vw)

    o = lax.map(one, jnp.arange(rows))
    return o.transpose(1, 0, 2, 3, 4).reshape(B, S, H, d)


def even_layer(x, norm, w_in, gq_a, gk_a, rpb, gq_b, gk_b, w_out, ang_row, ang_col):
    B, S, _ = x.shape
    h = rmsnorm(x, norm)
    qa, ka, va, qb, kb, vb, gate = _split(h @ w_in, SPLIT_E)
    qa = rmsnorm(qa.reshape(B, S, H_A, D_HEAD), gq_a)
    ka = rmsnorm(ka.reshape(B, S, H_A, D_HEAD), gk_a)
    va = va.reshape(B, S, H_A, D_HEAD)
    oa = neighbourhood_attention(qa, ka, va, rpb)
    half = D_HEAD // 2
    qb = rmsnorm(qb.reshape(B, S, H_B, D_HEAD), gq_b)
    kb = rmsnorm(kb.reshape(B, S, H_B_KV, D_HEAD), gk_b)
    qb = jnp.concatenate([rotate(qb[..., :half], ang_row), rotate(qb[..., half:], ang_col)], -1)
    kb = jnp.concatenate([rotate(kb[..., :half], ang_row), rotate(kb[..., half:], ang_col)], -1)
    vb = vb.reshape(B, S, H_B_KV, D_HEAD)
    ob = attend_blocks(qb, kb, vb, D_HEAD ** -0.5)
    o = jnp.concatenate([oa.reshape(B, S, -1), ob.reshape(B, S, -1)], -1) * jax.nn.silu(gate)
    return o @ w_out


def odd_layer(x, norm, w_in, g_cq, w_cq_b, g_ckv, w_ckv_b, gq_c, gk_c, gq_d, gk_d,
              lam_q1, lam_k1, lam_q2, lam_k2, g_sub_d, w_out, ang_mla, ang_part, lam_init):
    B, S, _ = x.shape
    h = rmsnorm(x, norm)
    cq, ckv, kpe, qd, kd, vd, gate = _split(h @ w_in, SPLIT_O)
    q = (rmsnorm(cq, g_cq) @ w_cq_b).reshape(B, S, H_C, C_NOPE + C_ROPE)
    kv = (rmsnorm(ckv, g_ckv) @ w_ckv_b).reshape(B, S, H_C, C_NOPE + C_V)
    k_nope, v_c = kv[..., :C_NOPE], kv[..., C_NOPE:]
    k_pe = jnp.broadcast_to(kpe[:, :, None, :], (B, S, H_C, C_ROPE))
    qc = rmsnorm(q, gq_c)
    kc = rmsnorm(jnp.concatenate([k_nope, k_pe], -1), gk_c)
    qc = jnp.concatenate([qc[..., :C_NOPE], rotate(qc[..., C_NOPE:], ang_mla)], -1)
    kc = jnp.concatenate([kc[..., :C_NOPE], rotate(kc[..., C_NOPE:], ang_mla)], -1)
    oc = attend_blocks(qc, kc, v_c, (C_NOPE + C_ROPE) ** -0.5)
    qd = rmsnorm(qd.reshape(B, S, H_D, 2, D_HEAD), gq_d)
    kd = rmsnorm(kd.reshape(B, S, H_D, 2, D_HEAD), gk_d)
    qd = jnp.concatenate([rotate(qd[..., :ROT_DIM], ang_part), qd[..., ROT_DIM:]], -1)
    kd = jnp.concatenate([rotate(kd[..., :ROT_DIM], ang_part), kd[..., ROT_DIM:]], -1)
    vd = vd.reshape(B, S, H_D, D_V)
    f32 = jnp.float32
    lam = (jnp.exp(jnp.sum(lam_q1.astype(f32) * lam_k1.astype(f32)))
           - jnp.exp(jnp.sum(lam_q2.astype(f32) * lam_k2.astype(f32))) + lam_init)
    od = diff_attend_blocks(qd[:, :, :, 0], qd[:, :, :, 1], kd[:, :, :, 0], kd[:, :, :, 1],
                            vd, lam, D_HEAD ** -0.5)
    od = rmsnorm(od, g_sub_d) * (1.0 - lam_init)
    o = jnp.concatenate([oc.reshape(B, S, -1), od.reshape(B, S, -1)], -1) * jax.nn.silu(gate)
    return o @ w_out


def setup_inputs(seed: int = 0) -> dict:
    key = jax.random.key(seed)
    ks = iter(jax.random.split(key, 32))
    f32 = jnp.float32

    def w(shape, fan_in):
        return jax.random.normal(next(ks), shape, f32) * (fan_in ** -0.5)

    def gain(shape):
        return 1.0 + 0.1 * jax.random.normal(next(ks), shape, f32)

    def small(shape, s):
        return s * jax.random.normal(next(ks), shape, f32)

    NE, NO = N_EVEN, N_ODD
    return {
        "x": jax.random.normal(next(ks), (BATCH, SEQ, D_MODEL), f32),
        "norm_e": gain((NE, D_MODEL)),
        "w_in_e": w((NE, D_MODEL, IN_E), D_MODEL),
        "gq_a": gain((NE, D_HEAD)),
        "gk_a": gain((NE, D_HEAD)),
        "rpb_a": small((NE, H_A, 2 * WIN_R - 1, 2 * WIN_C - 1), 0.2),
        "gq_b": gain((NE, D_HEAD)),
        "gk_b": gain((NE, D_HEAD)),
        "w_out_e": w((NE, GATE_E, D_MODEL), GATE_E),
        "norm_o": gain((NO, D_MODEL)),
        "w_in_o": w((NO, D_MODEL, IN_O), D_MODEL),
        "g_cq": gain((NO, Q_LORA)),
        "w_cq_b": w((NO, Q_LORA, H_C * (C_NOPE + C_ROPE)), Q_LORA),
        "g_ckv": gain((NO, KV_LORA)),
        "w_ckv_b": w((NO, KV_LORA, H_C * (C_NOPE + C_V)), KV_LORA),
        "gq_c": gain((NO, C_NOPE + C_ROPE)),
        "gk_c": gain((NO, C_NOPE + C_ROPE)),
        "gq_d": gain((NO, D_HEAD)),
        "gk_d": gain((NO, D_HEAD)),
        "lam_q1": small((NO, D_HEAD), 0.1),
        "lam_k1": small((NO, D_HEAD), 0.1),
        "lam_q2": small((NO, D_HEAD), 0.1),
        "lam_k2": small((NO, D_HEAD), 0.1),
        "g_sub_d": gain((NO, D_V)),
        "w_out_o": w((NO, GATE_O, D_MODEL), GATE_O),
    }


def reference(x, norm_e, w_in_e, gq_a, gk_a, rpb_a, gq_b, gk_b, w_out_e,
              norm_o, w_in_o, g_cq, w_cq_b, g_ckv, w_ckv_b, gq_c, gk_c, gq_d, gk_d,
              lam_q1, lam_k1, lam_q2, lam_k2, g_sub_d, w_out_o):
    S = x.shape[1]
    t = jnp.arange(S)
    pos = t.astype(jnp.float32)
    row = (t // GRID_W).astype(jnp.float32)
    col = (t % GRID_W).astype(jnp.float32)
    half = D_HEAD // 2
    ang_row = rope_angles(row, half, AXIAL_THETA)
    ang_col = rope_angles(col, half, AXIAL_THETA)
    ang_mla = rope_angles(pos, C_ROPE, MLA_THETA)
    ang_part = rope_angles(pos, ROT_DIM, ROPE_THETA)
    for l in range(DEPTH):
        i = l // 2
        if l % 2 == 0:
            x = x + even_layer(x, norm_e[i], w_in_e[i], gq_a[i], gk_a[i], rpb_a[i],
                               gq_b[i], gk_b[i], w_out_e[i], ang_row, ang_col)
        else:
            lam_init = 0.8 - 0.6 * math.exp(-0.3 * l)
            x = x + odd_layer(x, norm_o[i], w_in_o[i], g_cq[i], w_cq_b[i], g_ckv[i],
                              w_ckv_b[i], gq_c[i], gk_c[i], gq_d[i], gk_d[i],
                              lam_q1[i], lam_k1[i], lam_q2[i], lam_k2[i], g_sub_d[i],
                              w_out_o[i], ang_mla, ang_part, lam_init)
    return x
```

```python
import functools
import math

import jax
import jax.numpy as jnp
import numpy as np
from jax import lax
from jax.experimental import pallas as pl
from jax.experimental.pallas import tpu as pltpu

F32 = jnp.float32
BF16 = jnp.bfloat16

LANES = 128
VMEM_LIMIT_BYTES = 56 * 1024 * 1024

D_HEAD = 64
GRID_W = 64
RMS_EPS = 1e-6
WIN_R = 8
WIN_C = 16
AXIAL_THETA = 10000.0
MLA_THETA = 10000.0
ROPE_THETA = 500000.0
C_NOPE = 64
C_ROPE = 32
C_V = 64
ROT_DIM = 16
NEG_BIAS = -1e30

PROJ_TM = 512
PROJ_TN = 512
ATTN_TQ = 256
ATTN_TK = 512
PREP_ROWS = 512


def _params(*sem):
    return pltpu.CompilerParams(dimension_semantics=sem,
                                vmem_limit_bytes=VMEM_LIMIT_BYTES)


def _lane_iota(shape):
    return lax.broadcasted_iota(jnp.int32, shape, len(shape) - 1)


def _group_norm(x, gain, n_groups, n_eff):
    y = x * x
    tot = jnp.sum(y, axis=-1, keepdims=True)
    if n_groups == 1:
        ms = tot * (1.0 / n_eff)
    else:
        lo_mask = _lane_iota(x.shape) < D_HEAD
        lo = jnp.sum(jnp.where(lo_mask, y, 0.0), axis=-1, keepdims=True)
        ms = jnp.where(lo_mask, lo, tot - lo) * (1.0 / D_HEAD)
    return x * lax.rsqrt(ms + RMS_EPS) * gain


def _rotate(x, cos, sin_signed, shift):
    n = x.shape[-1]
    up = pltpu.roll(x, n - shift, 1)
    down = pltpu.roll(x, shift, 1)
    partner = jnp.where((_lane_iota(x.shape) & shift) == 0, up, down)
    return x * cos + partner * sin_signed


def _proj_in_kernel(x_ref, g_ref, w_ref, o_ref):
    x = x_ref[...]
    ms = jnp.mean(x * x, axis=-1, keepdims=True)
    h = (x * lax.rsqrt(ms + RMS_EPS) * g_ref[...]).astype(BF16)
    n = o_ref.shape[1]
    for n0 in range(0, n, PROJ_TN):
        o_ref[:, n0:n0 + PROJ_TN] = jnp.dot(
            h, w_ref[:, n0:n0 + PROJ_TN], preferred_element_type=F32)


def _proj_in(x2, gain, w):
    m, d = x2.shape
    n = w.shape[1]
    return pl.pallas_call(
        _proj_in_kernel,
        grid=(m // PROJ_TM,),
        in_specs=[pl.BlockSpec((PROJ_TM, d), lambda i: (i, 0)),
                  pl.BlockSpec((1, d), lambda i: (0, 0)),
                  pl.BlockSpec((d, n), lambda i: (0, 0))],
        out_specs=pl.BlockSpec((PROJ_TM, n), lambda i: (i, 0)),
        out_shape=jax.ShapeDtypeStruct((m, n), F32),
        compiler_params=_params("parallel"),
        name="proj_in",
    )(x2, gain.reshape(1, d), w)


def _proj_out_kernel(oa_ref, ob_ref, ga_ref, gb_ref, w_ref, x_ref, o_ref):
    def gated(o_r, g_r):
        g = g_r[...]
        return (o_r[...] * (g / (1.0 + jnp.exp(-g)))).astype(BF16)

    half = oa_ref.shape[1]
    y = jnp.dot(gated(oa_ref, ga_ref), w_ref[:half, :], preferred_element_type=F32)
    y = y + jnp.dot(gated(ob_ref, gb_ref), w_ref[half:, :], preferred_element_type=F32)
    o_ref[...] = x_ref[...] + y


def _proj_out(oa, ob, proj, gate_blk, w, x2):
    m, d = x2.shape
    half = oa.shape[1]
    return pl.pallas_call(
        _proj_out_kernel,
        grid=(m // PROJ_TM,),
        in_specs=[pl.BlockSpec((PROJ_TM, half), lambda i: (i, 0)),
                  pl.BlockSpec((PROJ_TM, half), lambda i: (i, 0)),
                  pl.BlockSpec((PROJ_TM, half), lambda i: (i, gate_blk)),
                  pl.BlockSpec((PROJ_TM, half), lambda i: (i, gate_blk + 1)),
                  pl.BlockSpec((2 * half, d), lambda i: (0, 0)),
                  pl.BlockSpec((PROJ_TM, d), lambda i: (i, 0))],
        out_specs=pl.BlockSpec((PROJ_TM, d), lambda i: (i, 0)),
        out_shape=jax.ShapeDtypeStruct((m, d), F32),
        compiler_params=_params("parallel"),
        name="proj_out",
    )(oa, ob, proj, proj, w, x2)


def _mla_up_kernel(cq_ref, ckv_ref, kpe_ref, gq_ref, gkv_ref, wq_ref, wk_ref, wv_ref,
                   q_ref, k_ref, v_ref):
    def normed(x, g):
        ms = jnp.mean(x * x, axis=-1, keepdims=True)
        return (x * lax.rsqrt(ms + RMS_EPS) * g).astype(BF16)

    hq = normed(cq_ref[...], gq_ref[...])
    hkv = normed(ckv_ref[...], gkv_ref[...])
    q_ref[...] = jnp.dot(hq, wq_ref[...], preferred_element_type=F32)
    v_ref[...] = jnp.dot(hkv, wv_ref[...], preferred_element_type=F32)
    kk = jnp.dot(hkv, wk_ref[...], preferred_element_type=F32)
    kpe = kpe_ref[...]
    for h in range(k_ref.shape[1] // LANES):
        k_ref[:, h * LANES:(h + 1) * LANES] = kk[:, h * LANES:(h + 1) * LANES] + kpe


def _mla_up(proj, g_cq, g_ckv, wq, wk, wv):
    m = proj.shape[0]
    ql, kvl = wq.shape[0], wk.shape[0]
    tm = PROJ_TM
    return pl.pallas_call(
        _mla_up_kernel,
        grid=(m // tm,),
        in_specs=[pl.BlockSpec((tm, ql), lambda i: (i, 0)),
                  pl.BlockSpec((tm, kvl), lambda i: (i, ql // kvl)),
                  pl.BlockSpec((tm, LANES), lambda i: (i, (ql + kvl) // LANES)),
                  pl.BlockSpec((1, ql), lambda i: (0, 0)),
                  pl.BlockSpec((1, kvl), lambda i: (0, 0)),
                  pl.BlockSpec(wq.shape, lambda i: (0, 0)),
                  pl.BlockSpec(wk.shape, lambda i: (0, 0)),
                  pl.BlockSpec(wv.shape, lambda i: (0, 0))],
        out_specs=[pl.BlockSpec((tm, wq.shape[1]), lambda i: (i, 0)),
                   pl.BlockSpec((tm, wk.shape[1]), lambda i: (i, 0)),
                   pl.BlockSpec((tm, wv.shape[1]), lambda i: (i, 0))],
        out_shape=[jax.ShapeDtypeStruct((m, wq.shape[1]), F32),
                   jax.ShapeDtypeStruct((m, wk.shape[1]), F32),
                   jax.ShapeDtypeStruct((m, wv.shape[1]), F32)],
        compiler_params=_params("parallel"),
        name="mla_up",
    )(proj, proj, proj, g_cq.reshape(1, ql), g_ckv.reshape(1, kvl), wq, wk, wv)


def _kprep_kernel(k_ref, g_ref, cos_ref, sin_ref, kt_ref, *, n_groups, n_eff, shift):
    s = k_ref.shape[0]
    for r0 in range(0, s, PREP_ROWS):
        k = _group_norm(k_ref[r0:r0 + PREP_ROWS, :], g_ref[...], n_groups, n_eff)
        k = _rotate(k, cos_ref[r0:r0 + PREP_ROWS, :], sin_ref[r0:r0 + PREP_ROWS, :], shift)
        kt_ref[0, 0, :, r0:r0 + PREP_ROWS] = k.T.astype(BF16)


def _kprep(src, blk0, n_units, batch, seq, gain, cos, sin, *, n_groups, n_eff, shift):
    return pl.pallas_call(
        functools.partial(_kprep_kernel, n_groups=n_groups, n_eff=n_eff, shift=shift),
        grid=(batch, n_units),
        in_specs=[pl.BlockSpec((seq, LANES), lambda b, u: (b, blk0 + u)),
                  pl.BlockSpec((1, LANES), lambda b, u: (0, 0)),
                  pl.BlockSpec((seq, LANES), lambda b, u: (0, 0)),
                  pl.BlockSpec((seq, LANES), lambda b, u: (0, 0))],
        out_specs=pl.BlockSpec((1, 1, LANES, seq), lambda b, u: (b, u, 0, 0)),
        out_shape=jax.ShapeDtypeStruct((batch, n_units, LANES, seq), BF16),
        compiler_params=_params("parallel", "parallel"),
        name="kprep",
    )(src, gain, cos, sin)


def _vprep_kernel(v_ref, o_ref):
    s = v_ref.shape[0]
    o_ref[0, 0, :, :LANES] = v_ref[...].astype(BF16)
    o_ref[0, 0, :, LANES:] = jnp.ones((s, LANES), BF16)


def _vprep(src, blk0, n_tiles, batch, seq):
    return pl.pallas_call(
        _vprep_kernel,
        grid=(batch, n_tiles),
        in_specs=[pl.BlockSpec((seq, LANES), lambda b, u: (b, blk0 + u))],
        out_specs=pl.BlockSpec((1, 1, seq, 2 * LANES), lambda b, u: (b, u, 0, 0)),
        out_shape=jax.ShapeDtypeStruct((batch, n_tiles, seq, 2 * LANES), BF16),
        compiler_params=_params("parallel", "parallel"),
        name="vprep",
    )(src)


def _attn_kernel(qe_ref, qo_ref, gq_ref, cos_ref, sin_ref, kte_ref, kto_ref, v_ref,
                 coef_ref, gpost_ref, o_ref, s_scr, *, n_groups, n_eff, shift, scale,
                 split_q, post_norm):
    seq = v_ref.shape[2]
    tq = qe_ref.shape[0]
    lane = _lane_iota((tq, LANES))
    res = []
    for u, (q_ref, kt_ref) in enumerate(((qe_ref, kte_ref), (qo_ref, kto_ref))):
        q = _group_norm(q_ref[...], gq_ref[...], n_groups, n_eff)
        q = _rotate(q, cos_ref[...], sin_ref[...], shift)
        if split_q:
            q = jnp.where((lane < D_HEAD) == (u == 0), q, 0.0)
        q = q.astype(BF16)
        m = jnp.full((tq, LANES), -jnp.inf, F32)
        for c0 in range(0, seq, ATTN_TK):
            s = jnp.dot(q, kt_ref[0, 0, :, c0:c0 + ATTN_TK], preferred_element_type=F32)
            s_scr[:, c0:c0 + ATTN_TK] = s
            for j in range(0, ATTN_TK, LANES):
                m = jnp.maximum(m, s[:, j:j + LANES])
        mrow = jnp.max(m, axis=-1, keepdims=True)
        acc = jnp.zeros((tq, 2 * LANES), F32)
        for c0 in range(0, seq, ATTN_TK):
            p = jnp.exp((s_scr[:, c0:c0 + ATTN_TK] - mrow) * scale).astype(BF16)
            acc = acc + jnp.dot(p, v_ref[0, 0, c0:c0 + ATTN_TK, :],
                                preferred_element_type=F32)
        res.append(acc[:, :LANES] / acc[:, LANES:])
    out = res[0] * coef_ref[0:1, :] + res[1] * coef_ref[1:2, :]
    if post_norm:
        ms = jnp.mean(out * out, axis=-1, keepdims=True)
        out = out * lax.rsqrt(ms + RMS_EPS) * gpost_ref[...]
    o_ref[...] = out


def _attention(q_src, q_blk, kt, kt_idx, vp, v_idx, gq, cos, sin, coef, gpost, *,
               batch, seq, n_pairs, n_groups, n_eff, shift, scale, split_q, post_norm):
    tq = ATTN_TQ
    nq = seq // tq
    m = batch * seq

    def qmap(which):
        return lambda b, p, i: (b * nq + i, q_blk(p)[which])

    def kmap(which):
        return lambda b, p, i: (b, kt_idx(p)[which], 0, 0)

    kern = functools.partial(_attn_kernel, n_groups=n_groups, n_eff=n_eff, shift=shift,
                             scale=scale, split_q=split_q, post_norm=post_norm)
    return pl.pallas_call(
        kern,
        grid=(batch, n_pairs, nq),
        in_specs=[pl.BlockSpec((tq, LANES), qmap(0)),
                  pl.BlockSpec((tq, LANES), qmap(1)),
                  pl.BlockSpec((1, LANES), lambda b, p, i: (0, 0)),
                  pl.BlockSpec((tq, LANES), lambda b, p, i: (i, 0)),
                  pl.BlockSpec((tq, LANES), lambda b, p, i: (i, 0)),
                  pl.BlockSpec((1, 1, LANES, seq), kmap(0)),
                  pl.BlockSpec((1, 1, LANES, seq), kmap(1)),
                  pl.BlockSpec((1, 1, seq, 2 * LANES), lambda b, p, i: (b, v_idx(p), 0, 0)),
                  pl.BlockSpec((8, LANES), lambda b, p, i: (0, 0)),
                  pl.BlockSpec((1, LANES), lambda b, p, i: (0, 0))],
        out_specs=pl.BlockSpec((tq, LANES), lambda b, p, i: (b * nq + i, p)),
        out_shape=jax.ShapeDtypeStruct((m, n_pairs * LANES), F32),
        scratch_shapes=[pltpu.VMEM((tq, seq), F32)],
        compiler_params=_params("parallel", "parallel", "parallel"),
        name="attention",
    )(q_src, q_src, gq, cos, sin, kt, kt, vp, coef, gpost)


def _natten_kernel(q_ref, k_ref, v_ref, gq_ref, gk_ref, bias_ref, o_ref,
                   q_scr, k_scr, v_scr, *, rows, scale):
    seq = q_ref.shape[0]
    nwin = WIN_R * GRID_W
    for r0 in range(0, seq, PREP_ROWS):
        sl = slice(r0, r0 + PREP_ROWS)
        q = _group_norm(q_ref[sl, :], gq_ref[...], 2, D_HEAD)
        lo = _lane_iota(q.shape) < D_HEAD
        q_scr[0, sl, :] = jnp.where(lo, q, 0.0).astype(BF16)
        q_scr[1, sl, :] = jnp.where(lo, 0.0, q).astype(BF16)
        k_scr[sl, :] = _group_norm(k_ref[sl, :], gk_ref[...], 2, D_HEAD).astype(BF16)
        v_scr[sl, :LANES] = v_ref[sl, :].astype(BF16)
        v_scr[sl, LANES:] = jnp.ones((PREP_ROWS, LANES), BF16)

    lane_lo = _lane_iota((GRID_W, LANES)) < D_HEAD

    def body(r, carry):
        w0 = jnp.clip(r - WIN_R // 2, 0, rows - WIN_R)
        var = r - w0
        kbase = pl.multiple_of(w0 * GRID_W, GRID_W)
        qbase = pl.multiple_of(r * GRID_W, GRID_W)
        kw = k_scr[pl.ds(kbase, nwin), :]
        vw = v_scr[pl.ds(kbase, nwin), :]
        res = []
        for h in range(2):
            qr = q_scr[h, pl.ds(qbase, GRID_W), :]
            s = lax.dot_general(qr, kw, (((1,), (1,)), ((), ())),
                                preferred_element_type=F32)
            z = s * scale + bias_ref[h, var]
            zmax = jnp.max(z, axis=-1, keepdims=True)
            p = jnp.exp(z - zmax).astype(BF16)
            acc = jnp.dot(p, vw, preferred_element_type=F32)
            res.append(acc[:, :LANES] / acc[:, LANES:])
        o_ref[pl.ds(qbase, GRID_W), :] = jnp.where(lane_lo, res[0], res[1])
        return carry

    lax.fori_loop(0, rows, body, 0)


def _natten(proj, gq, gk, bias, batch, seq):
    rows = seq // GRID_W
    n_pairs = bias.shape[0] // 2
    nwin = WIN_R * GRID_W
    kern = functools.partial(_natten_kernel, rows=rows, scale=D_HEAD ** -0.5)
    return pl.pallas_call(
        kern,
        grid=(batch, n_pairs),
        in_specs=[pl.BlockSpec((seq, LANES), lambda b, p: (b, p)),
                  pl.BlockSpec((seq, LANES), lambda b, p: (b, n_pairs + p)),
                  pl.BlockSpec((seq, LANES), lambda b, p: (b, 2 * n_pairs + p)),
                  pl.BlockSpec((1, LANES), lambda b, p: (0, 0)),
                  pl.BlockSpec((1, LANES), lambda b, p: (0, 0)),
                  pl.BlockSpec((2, WIN_R, GRID_W, nwin), lambda b, p: (p, 0, 0, 0))],
        out_specs=pl.BlockSpec((seq, LANES), lambda b, p: (b, p)),
        out_shape=jax.ShapeDtypeStruct((batch * seq, n_pairs * LANES), F32),
        scratch_shapes=[pltpu.VMEM((2, seq, LANES), BF16),
                        pltpu.VMEM((seq, LANES), BF16),
                        pltpu.VMEM((seq, 2 * LANES), BF16)],
        compiler_params=_params("parallel", "parallel"),
        name="natten",
    )(proj, proj, proj, gq, gk, bias)


def _natten_bias(rpb):
    col = np.arange(GRID_W)
    c0 = np.clip(col - WIN_C // 2, 0, GRID_W - WIN_C)
    ci = np.arange(GRID_W)
    inside = (ci[None, :] >= c0[:, None]) & (ci[None, :] < c0[:, None] + WIN_C)
    cb = np.clip(ci[None, :] - col[:, None] + (WIN_C - 1), 0, 2 * WIN_C - 2)
    var = np.arange(WIN_R)
    rb = np.arange(WIN_R)[None, :] - var[:, None] + (WIN_R - 1)
    t = rpb[:, rb[:, None, :, None], cb[None, :, None, :]]
    t = jnp.where(inside[None, None, :, None, :], t.astype(F32), NEG_BIAS)
    return t.reshape(rpb.shape[0], WIN_R, GRID_W, WIN_R * GRID_W)


def _angles(pos, dim, theta):
    inv = jnp.power(theta, -jnp.arange(0, dim, 2, dtype=F32) / dim)
    return pos[:, None] * inv[None, :]


def _rot_tables(parts, seq, width):
    cos = jnp.ones((seq, width), F32)
    sin = jnp.zeros((seq, width), F32)
    for off, ang in parts:
        n = ang.shape[1]
        c, s = jnp.cos(ang), jnp.sin(ang)
        cos = cos.at[:, off:off + n].set(c).at[:, off + n:off + 2 * n].set(c)
        sin = sin.at[:, off:off + n].set(-s).at[:, off + n:off + 2 * n].set(s)
    return cos, sin


def _dup(v):
    return jnp.concatenate([v, v]).reshape(1, -1).astype(F32)


def _pad_lanes(v):
    return jnp.pad(v.astype(F32), (0, LANES - v.shape[0])).reshape(1, LANES)


def kernel(x, norm_e, w_in_e, gq_a, gk_a, rpb_a, gq_b, gk_b, w_out_e, norm_o, w_in_o, g_cq, w_cq_b, g_ckv, w_ckv_b, gq_c, gk_c, gq_d, gk_d, lam_q1, lam_k1, lam_q2, lam_k2, g_sub_d, w_out_o):
    batch, seq, d = x.shape
    m = batch * seq
    x2 = x.reshape(m, d)
    t = jnp.arange(seq)
    pos = t.astype(F32)
    row = (t // GRID_W).astype(F32)
    col = (t % GRID_W).astype(F32)
    half = D_HEAD // 2
    ones_row = jnp.ones((1, LANES), F32)
    pair_coef = jnp.zeros((8, LANES), F32)
    pair_coef = pair_coef.at[0, :D_HEAD].set(1.0).at[1, D_HEAD:].set(1.0)

    n_a = 8 * D_HEAD
    base_b = 3 * n_a
    kb0 = base_b + 8 * D_HEAD
    vb0 = kb0 + 2 * D_HEAD
    gate0 = vb0 + 2 * D_HEAD
    w0 = w_in_e[0]
    dup_heads = [w0[:, c:c + D_HEAD] for c in (kb0, kb0, kb0 + D_HEAD, kb0 + D_HEAD,
                                                 vb0, vb0, vb0 + D_HEAD, vb0 + D_HEAD)]
    w_e = jnp.concatenate([w0[:, :kb0]] + dup_heads + [w0[:, gate0:]],
                          axis=1).astype(BF16)
    proj_e = _proj_in(x2, norm_e[0], w_e)
    bias = _natten_bias(rpb_a[0])
    oa = _natten(proj_e, _dup(gq_a[0]), _dup(gk_a[0]), bias, batch, seq)
    ang_row = _angles(row, half, AXIAL_THETA)
    ang_col = _angles(col, half, AXIAL_THETA)
    cos_b, sin_b = _rot_tables([(0, ang_row), (half, ang_col),
                                (D_HEAD, ang_row), (D_HEAD + half, ang_col)], seq, LANES)
    kt_b = _kprep(proj_e, 16, 2, batch, seq, _dup(gk_b[0]), cos_b, sin_b,
                  n_groups=2, n_eff=D_HEAD, shift=half // 2)
    vp_b = _vprep(proj_e, 18, 2, batch, seq)
    ob = _attention(proj_e, lambda p: (12 + p, 12 + p), kt_b, lambda p: (p // 2, p // 2),
                    vp_b, lambda p: p // 2, _dup(gq_b[0]), cos_b, sin_b, pair_coef, ones_row,
                    batch=batch, seq=seq, n_pairs=4, n_groups=2, n_eff=D_HEAD,
                    shift=half // 2, scale=D_HEAD ** -0.5, split_q=True, post_norm=False)
    x2 = _proj_out(oa, ob, proj_e, 5, w_out_e[0].astype(BF16), x2)

    w1 = w_in_o[0]
    q_lora, kv_lora = g_cq.shape[1], g_ckv.shape[1]
    o_kpe = q_lora + kv_lora
    o_qd = o_kpe + C_ROPE
    zeros = functools.partial(jnp.zeros, dtype=w1.dtype)
    w_o = jnp.concatenate([
        w1[:, :o_kpe],
        zeros((d, C_NOPE)), w1[:, o_kpe:o_qd], zeros((d, LANES - C_NOPE - C_ROPE)),
        w1[:, o_qd:]], axis=1).astype(BF16)
    proj_o = _proj_in(x2, norm_o[0], w_o)
    n_hc = 8
    dqc = C_NOPE + C_ROPE
    wq = jnp.pad(w_cq_b[0].reshape(q_lora, n_hc, dqc),
                 ((0, 0), (0, 0), (0, LANES - dqc))).reshape(q_lora, n_hc * LANES).astype(BF16)
    wkv = w_ckv_b[0].reshape(kv_lora, n_hc, C_NOPE + C_V)
    wk = jnp.pad(wkv[:, :, :C_NOPE],
                 ((0, 0), (0, 0), (0, LANES - C_NOPE))).reshape(kv_lora, n_hc * LANES).astype(BF16)
    wv = wkv[:, :, C_NOPE:].reshape(kv_lora, n_hc * C_V).astype(BF16)
    q_c, k_c, v_c = _mla_up(proj_o, g_cq[0], g_ckv[0], wq, wk, wv)
    ang_mla = _angles(pos, C_ROPE, MLA_THETA)
    cos_c, sin_c = _rot_tables([(C_NOPE, ang_mla)], seq, LANES)
    kt_c = _kprep(k_c, 0, n_hc, batch, seq, _pad_lanes(gk_c[0]), cos_c, sin_c,
                  n_groups=1, n_eff=dqc, shift=C_ROPE // 2)
    vp_c = _vprep(v_c, 0, n_hc // 2, batch, seq)
    oc = _attention(q_c, lambda p: (2 * p, 2 * p + 1), kt_c, lambda p: (2 * p, 2 * p + 1),
                    vp_c, lambda p: p, _pad_lanes(gq_c[0]), cos_c, sin_c, pair_coef, ones_row,
                    batch=batch, seq=seq, n_pairs=4, n_groups=1, n_eff=dqc,
                    shift=C_ROPE // 2, scale=dqc ** -0.5, split_q=False, post_norm=False)

    ang_part = _angles(pos, ROT_DIM, ROPE_THETA)
    cos_d, sin_d = _rot_tables([(0, ang_part), (D_HEAD, ang_part)], seq, LANES)
    lam_init = 0.8 - 0.6 * math.exp(-0.3 * 1)
    lam = (jnp.exp(jnp.sum(lam_q1[0].astype(F32) * lam_k1[0].astype(F32)))
           - jnp.exp(jnp.sum(lam_q2[0].astype(F32) * lam_k2[0].astype(F32))) + lam_init)
    diff_coef = jnp.zeros((8, LANES), F32).at[0].set(1.0).at[1].set(-lam)
    gpost = (g_sub_d[0].astype(F32) * (1.0 - lam_init)).reshape(1, LANES)
    kt_d = _kprep(proj_o, 8, 4, batch, seq, _dup(gk_d[0]), cos_d, sin_d,
                  n_groups=2, n_eff=D_HEAD, shift=ROT_DIM // 2)
    vp_d = _vprep(proj_o, 12, 4, batch, seq)
    od = _attention(proj_o, lambda p: (4 + p, 4 + p), kt_d, lambda p: (p, p),
                    vp_d, lambda p: p, _dup(gq_d[0]), cos_d, sin_d, diff_coef, gpost,
                    batch=batch, seq=seq, n_pairs=4, n_groups=2, n_eff=D_HEAD,
                    shift=ROT_DIM // 2, scale=D_HEAD ** -0.5, split_q=True, post_norm=True)
    x2 = _proj_out(oc, od, proj_o, 4, w_out_o[0].astype(BF16), x2)
    return x2.reshape(batch, seq, d)
```

```python
import functools
import math

import jax
import jax.numpy as jnp
import numpy as np
from jax import lax
from jax.experimental import pallas as pl
from jax.experimental.pallas import tpu as pltpu

F32 = jnp.float32
BF16 = jnp.bfloat16

LANES = 128
VMEM_LIMIT_BYTES = 56 * 1024 * 1024

D_HEAD = 64
GRID_W = 64
RMS_EPS = 1e-6
WIN_R = 8
WIN_C = 16
AXIAL_THETA = 10000.0
MLA_THETA = 10000.0
ROPE_THETA = 500000.0
C_NOPE = 64
C_ROPE = 32
C_V = 64
ROT_DIM = 16
NEG_BIAS = -1e30

PROJ_TM = 512
PROJ_TN = 512
ATTN_TQ = 256
ATTN_TK = 512
PREP_ROWS = 512


def _params(*sem):
    return pltpu.CompilerParams(dimension_semantics=sem,
                                vmem_limit_bytes=VMEM_LIMIT_BYTES)


def _lane_iota(shape):
    return lax.broadcasted_iota(jnp.int32, shape, len(shape) - 1)


def _group_norm(x, gain, n_groups, n_eff):
    y = x * x
    tot = jnp.sum(y, axis=-1, keepdims=True)
    if n_groups == 1:
        ms = tot * (1.0 / n_eff)
    else:
        lo_mask = _lane_iota(x.shape) < D_HEAD
        lo = jnp.sum(jnp.where(lo_mask, y, 0.0), axis=-1, keepdims=True)
        ms = jnp.where(lo_mask, lo, tot - lo) * (1.0 / D_HEAD)
    return x * lax.rsqrt(ms + RMS_EPS) * gain


def _rotate(x, cos, sin_signed, shift):
    n = x.shape[-1]
    up = pltpu.roll(x, n - shift, 1)
    down = pltpu.roll(x, shift, 1)
    partner = jnp.where((_lane_iota(x.shape) & shift) == 0, up, down)
    return x * cos + partner * sin_signed


def _proj_in_kernel(x_ref, g_ref, w_ref, o_ref):
    x = x_ref[...]
    ms = jnp.mean(x * x, axis=-1, keepdims=True)
    h = (x * lax.rsqrt(ms + RMS_EPS) * g_ref[...]).astype(BF16)
    n = o_ref.shape[1]
    for n0 in range(0, n, PROJ_TN):
        o_ref[:, n0:n0 + PROJ_TN] = jnp.dot(
            h, w_ref[:, n0:n0 + PROJ_TN], preferred_element_type=F32)


def _proj_in(x2, gain, w):
    m, d = x2.shape
    n = w.shape[1]
    return pl.pallas_call(
        _proj_in_kernel,
        grid=(m // PROJ_TM,),
        in_specs=[pl.BlockSpec((PROJ_TM, d), lambda i: (i, 0)),
                  pl.BlockSpec((1, d), lambda i: (0, 0)),
                  pl.BlockSpec((d, n), lambda i: (0, 0))],
        out_specs=pl.BlockSpec((PROJ_TM, n), lambda i: (i, 0)),
        out_shape=jax.ShapeDtypeStruct((m, n), F32),
        compiler_params=_params("parallel"),
        name="proj_in",
    )(x2, gain.reshape(1, d), w)


def _proj_out_kernel(oa_ref, ob_ref, ga_ref, gb_ref, w_ref, x_ref, o_ref):
    def gated(o_r, g_r):
        g = g_r[...]
        return (o_r[...] * (g / (1.0 + jnp.exp(-g)))).astype(BF16)

    half = oa_ref.shape[1]
    y = jnp.dot(gated(oa_ref, ga_ref), w_ref[:half, :], preferred_element_type=F32)
    y = y + jnp.dot(gated(ob_ref, gb_ref), w_ref[half:, :], preferred_element_type=F32)
    o_ref[...] = x_ref[...] + y


def _proj_out(oa, ob, proj, gate_blk, w, x2):
    m, d = x2.shape
    half = oa.shape[1]
    return pl.pallas_call(
        _proj_out_kernel,
        grid=(m // PROJ_TM,),
        in_specs=[pl.BlockSpec((PROJ_TM, half), lambda i: (i, 0)),
                  pl.BlockSpec((PROJ_TM, half), lambda i: (i, 0)),
                  pl.BlockSpec((PROJ_TM, half), lambda i: (i, gate_blk)),
                  pl.BlockSpec((PROJ_TM, half), lambda i: (i, gate_blk + 1)),
                  pl.BlockSpec((2 * half, d), lambda i: (0, 0)),
                  pl.BlockSpec((PROJ_TM, d), lambda i: (i, 0))],
        out_specs=pl.BlockSpec((PROJ_TM, d), lambda i: (i, 0)),
        out_shape=jax.ShapeDtypeStruct((m, d), F32),
        compiler_params=_params("parallel"),
        name="proj_out",
    )(oa, ob, proj, proj, w, x2)


def _mla_up_kernel(cq_ref, ckv_ref, kpe_ref, gq_ref, gkv_ref, wq_ref, wk_ref, wv_ref,
                   q_ref, k_ref, v_ref):
    def normed(x, g):
        ms = jnp.mean(x * x, axis=-1, keepdims=True)
        return (x * lax.rsqrt(ms + RMS_EPS) * g).astype(BF16)

    hq = normed(cq_ref[...], gq_ref[...])
    hkv = normed(ckv_ref[...], gkv_ref[...])
    q_ref[...] = jnp.dot(hq, wq_ref[...], preferred_element_type=F32)
    v_ref[...] = jnp.dot(hkv, wv_ref[...], preferred_element_type=F32)
    kk = jnp.dot(hkv, wk_ref[...], preferred_element_type=F32)
    kpe = kpe_ref[...]
    for h in range(k_ref.shape[1] // LANES):
        k_ref[:, h * LANES:(h + 1) * LANES] = kk[:, h * LANES:(h + 1) * LANES] + kpe


def _mla_up(proj, g_cq, g_ckv, wq, wk, wv):
    m = proj.shape[0]
    ql, kvl = wq.shape[0], wk.shape[0]
    tm = PROJ_TM
    return pl.pallas_call(
        _mla_up_kernel,
        grid=(m // tm,),
        in_specs=[pl.BlockSpec((tm, ql), lambda i: (i, 0)),
                  pl.BlockSpec((tm, kvl), lambda i: (i, ql // kvl)),
                  pl.BlockSpec((tm, LANES), lambda i: (i, (ql + kvl) // LANES)),
                  pl.BlockSpec((1, ql), lambda i: (0, 0)),
                  pl.BlockSpec((1, kvl), lambda i: (0, 0)),
                  pl.BlockSpec(wq.shape, lambda i: (0, 0)),
                  pl.BlockSpec(wk.shape, lambda i: (0, 0)),
                  pl.BlockSpec(wv.shape, lambda i: (0, 0))],
        out_specs=[pl.BlockSpec((tm, wq.shape[1]), lambda i: (i, 0)),
                   pl.BlockSpec((tm, wk.shape[1]), lambda i: (i, 0)),
                   pl.BlockSpec((tm, wv.shape[1]), lambda i: (i, 0))],
        out_shape=[jax.ShapeDtypeStruct((m, wq.shape[1]), F32),
                   jax.ShapeDtypeStruct((m, wk.shape[1]), F32),
                   jax.ShapeDtypeStruct((m, wv.shape[1]), F32)],
        compiler_params=_params("parallel"),
        name="mla_up",
    )(proj, proj, proj, g_cq.reshape(1, ql), g_ckv.reshape(1, kvl), wq, wk, wv)


def _kprep_kernel(k_ref, g_ref, cos_ref, sin_ref, kn_ref, *, n_groups, n_eff, shift):
    s = k_ref.shape[0]
    for r0 in range(0, s, PREP_ROWS):
        k = _group_norm(k_ref[r0:r0 + PREP_ROWS, :], g_ref[...], n_groups, n_eff)
        k = _rotate(k, cos_ref[r0:r0 + PREP_ROWS, :], sin_ref[r0:r0 + PREP_ROWS, :], shift)
        kn_ref[0, 0, r0:r0 + PREP_ROWS, :] = k.astype(BF16)


def _kprep(src, blk0, n_units, batch, seq, gain, cos, sin, *, n_groups, n_eff, shift):
    return pl.pallas_call(
        functools.partial(_kprep_kernel, n_groups=n_groups, n_eff=n_eff, shift=shift),
        grid=(batch, n_units),
        in_specs=[pl.BlockSpec((seq, LANES), lambda b, u: (b, blk0 + u)),
                  pl.BlockSpec((1, LANES), lambda b, u: (0, 0)),
                  pl.BlockSpec((seq, LANES), lambda b, u: (0, 0)),
                  pl.BlockSpec((seq, LANES), lambda b, u: (0, 0))],
        out_specs=pl.BlockSpec((1, 1, seq, LANES), lambda b, u: (b, u, 0, 0)),
        out_shape=jax.ShapeDtypeStruct((batch, n_units, seq, LANES), BF16),
        compiler_params=_params("parallel", "parallel"),
        name="kprep",
    )(src, gain, cos, sin)


def _vprep_kernel(v_ref, o_ref):
    s = v_ref.shape[0]
    for r0 in range(0, s, PREP_ROWS):
        o_ref[0, 0, :, r0:r0 + PREP_ROWS] = v_ref[r0:r0 + PREP_ROWS, :].T.astype(BF16)


def _vprep(src, blk0, n_tiles, batch, seq):
    return pl.pallas_call(
        _vprep_kernel,
        grid=(batch, n_tiles),
        in_specs=[pl.BlockSpec((seq, LANES), lambda b, u: (b, blk0 + u))],
        out_specs=pl.BlockSpec((1, 1, LANES, seq), lambda b, u: (b, u, 0, 0)),
        out_shape=jax.ShapeDtypeStruct((batch, n_tiles, LANES, seq), BF16),
        compiler_params=_params("parallel", "parallel"),
        name="vprep",
    )(src)


def _attn_kernel(qe_ref, qo_ref, gq_ref, cos_ref, sin_ref, ke_ref, ko_ref, vt_ref,
                 coef_ref, gpost_ref, o_ref, s0_scr, s1_scr, p0_scr, p1_scr, qt_scr, res_scr,
                 *, n_groups, n_eff, shift, qscale, split_q, combine, post_norm):
    seq = vt_ref.shape[3]
    tq = ATTN_TQ
    nq = seq // tq
    q_refs, k_refs = (qe_ref, qo_ref), (ke_ref, ko_ref)
    s_scrs, p_scrs = (s0_scr, s1_scr), (p0_scr, p1_scr)
    lane = _lane_iota((tq, LANES))

    def block_rows(i):
        if isinstance(i, int):
            return pl.ds(i * tq, tq)
        return pl.ds(pl.multiple_of(i * tq, tq), tq)

    def fold8(x, op):
        return op(x.reshape(x.shape[0] // 8, 8, x.shape[1]), axis=0)

    def prep_q(i, u):
        rows = block_rows(i)
        q = _group_norm(q_refs[u][rows, :], gq_ref[...], n_groups, n_eff)
        q = _rotate(q, cos_ref[rows, :], sin_ref[rows, :], shift) * qscale
        if split_q:
            q = jnp.where((lane < D_HEAD) == (u == 0), q, 0.0)
        qt_scr[u] = q.T.astype(BF16)

    def tick(scores=None, exps=None, values=None):
        m8 = l8 = acc = None
        if scores is not None:
            qt = qt_scr[scores]
        for c0 in range(0, seq, ATTN_TK):
            ks = slice(c0, c0 + ATTN_TK)
            if values is not None:
                uv, l = values
                pv = jnp.dot(vt_ref[0, 0, :, ks], p_scrs[uv][ks, :],
                             preferred_element_type=F32)
                acc = pv if acc is None else acc + pv
            if scores is not None:
                s = jnp.dot(k_refs[scores][0, 0, ks, :], qt, preferred_element_type=F32)
                s_scrs[scores][ks, :] = s
                smax = fold8(s, jnp.max)
                m8 = smax if m8 is None else jnp.maximum(m8, smax)
            if exps is not None:
                ue, m = exps
                p = jnp.exp2(s_scrs[ue][ks, :] - m)
                p_scrs[ue][ks, :] = p.astype(BF16)
                psum = fold8(p, jnp.sum)
                l8 = psum if l8 is None else l8 + psum
        m_new = None if m8 is None else jnp.max(m8, axis=0, keepdims=True)
        l_new = None if l8 is None else jnp.sum(l8, axis=0, keepdims=True)
        res = None if acc is None else acc * (1.0 / values[1])
        return m_new, l_new, res

    def emit(i, res_e, res_o):
        if combine == "select":
            sub = lax.broadcasted_iota(jnp.int32, (LANES, tq), 0)
            out_t = jnp.where(sub < D_HEAD, res_e, res_o)
        else:
            out_t = res_e + coef_ref[0:1, :] * res_o
        if post_norm:
            ms = jnp.mean(out_t * out_t, axis=0, keepdims=True)
            out_t = out_t * lax.rsqrt(ms + RMS_EPS)
        o_ref[block_rows(i), :] = out_t.T * gpost_ref[...]

    prep_q(0, 0)
    prep_q(0, 1)
    res_scr[...] = jnp.zeros(res_scr.shape, F32)
    m_e, _, _ = tick(scores=0)
    prep_q(1, 0)
    m_o, l_e, _ = tick(scores=1, exps=(0, m_e))
    prep_q(1, 1)

    def body(i, carry):
        m_o, l_e = carry
        emit(jnp.maximum(i - 1, 0), res_scr[0], res_scr[1])
        m_e_next, l_o, res_e = tick(scores=0, exps=(1, m_o), values=(0, l_e))
        ahead = jnp.minimum(i + 2, nq - 1)
        prep_q(ahead, 0)
        m_o_next, l_e_next, res_o = tick(scores=1, exps=(0, m_e_next), values=(1, l_o))
        prep_q(ahead, 1)
        res_scr[0] = res_e
        res_scr[1] = res_o
        return m_o_next, l_e_next

    m_o, l_e = lax.fori_loop(0, nq - 1, body, (m_o, l_e))
    emit(max(nq - 2, 0), res_scr[0], res_scr[1])
    _, l_o, res_e = tick(exps=(1, m_o), values=(0, l_e))
    _, _, res_o = tick(values=(1, l_o))
    emit(nq - 1, res_e, res_o)


def _attention(q_src, q_blk, kn, k_idx, vt, v_idx, gq, cos, sin, coef, gpost, *,
               batch, seq, n_pairs, n_groups, n_eff, shift, scale, split_q, combine,
               post_norm):
    tq = ATTN_TQ
    m = batch * seq

    def qmap(which):
        return lambda b, p: (b, q_blk(p)[which])

    def kmap(which):
        return lambda b, p: (b, k_idx(p)[which], 0, 0)

    kern = functools.partial(_attn_kernel, n_groups=n_groups, n_eff=n_eff, shift=shift,
                             qscale=scale * math.log2(math.e), split_q=split_q,
                             combine=combine, post_norm=post_norm)
    return pl.pallas_call(
        kern,
        grid=(batch, n_pairs),
        in_specs=[pl.BlockSpec((seq, LANES), qmap(0)),
                  pl.BlockSpec((seq, LANES), qmap(1)),
                  pl.BlockSpec((1, LANES), lambda b, p: (0, 0)),
                  pl.BlockSpec((seq, LANES), lambda b, p: (0, 0)),
                  pl.BlockSpec((seq, LANES), lambda b, p: (0, 0)),
                  pl.BlockSpec((1, 1, seq, LANES), kmap(0)),
                  pl.BlockSpec((1, 1, seq, LANES), kmap(1)),
                  pl.BlockSpec((1, 1, LANES, seq), lambda b, p: (b, v_idx(p), 0, 0)),
                  pl.BlockSpec((8, tq), lambda b, p: (0, 0)),
                  pl.BlockSpec((1, LANES), lambda b, p: (0, 0))],
        out_specs=pl.BlockSpec((seq, LANES), lambda b, p: (b, p)),
        out_shape=jax.ShapeDtypeStruct((m, n_pairs * LANES), F32),
        scratch_shapes=[pltpu.VMEM((seq, tq), F32), pltpu.VMEM((seq, tq), F32),
                        pltpu.VMEM((seq, tq), BF16), pltpu.VMEM((seq, tq), BF16),
                        pltpu.VMEM((2, LANES, tq), BF16), pltpu.VMEM((2, LANES, tq), F32)],
        compiler_params=_params("parallel", "parallel"),
        name="attention",
    )(q_src, q_src, gq, cos, sin, kn, kn, vt, coef, gpost)


def _natten_kernel(q_ref, k_ref, v_ref, gq_ref, gk_ref, bias_ref, o_ref,
                   q_scr, k_scr, v_scr, *, rows, scale):
    seq = q_ref.shape[0]
    nwin = WIN_R * GRID_W
    for r0 in range(0, seq, PREP_ROWS):
        sl = slice(r0, r0 + PREP_ROWS)
        q = _group_norm(q_ref[sl, :], gq_ref[...], 2, D_HEAD)
        lo = _lane_iota(q.shape) < D_HEAD
        q_scr[0, sl, :] = jnp.where(lo, q, 0.0).astype(BF16)
        q_scr[1, sl, :] = jnp.where(lo, 0.0, q).astype(BF16)
        k_scr[sl, :] = _group_norm(k_ref[sl, :], gk_ref[...], 2, D_HEAD).astype(BF16)
        v_scr[sl, :LANES] = v_ref[sl, :].astype(BF16)
        v_scr[sl, LANES:] = jnp.ones((PREP_ROWS, LANES), BF16)

    lane_lo = _lane_iota((GRID_W, LANES)) < D_HEAD

    def body(r, carry):
        w0 = jnp.clip(r - WIN_R // 2, 0, rows - WIN_R)
        var = r - w0
        kbase = pl.multiple_of(w0 * GRID_W, GRID_W)
        qbase = pl.multiple_of(r * GRID_W, GRID_W)
        kw = k_scr[pl.ds(kbase, nwin), :]
        vw = v_scr[pl.ds(kbase, nwin), :]
        res = []
        for h in range(2):
            qr = q_scr[h, pl.ds(qbase, GRID_W), :]
            s = lax.dot_general(qr, kw, (((1,), (1,)), ((), ())),
                                preferred_element_type=F32)
            z = s * scale + bias_ref[h, var]
            zmax = jnp.max(z, axis=-1, keepdims=True)
            p = jnp.exp(z - zmax).astype(BF16)
            acc = jnp.dot(p, vw, preferred_element_type=F32)
            res.append(acc[:, :LANES] / acc[:, LANES:])
        o_ref[pl.ds(qbase, GRID_W), :] = jnp.where(lane_lo, res[0], res[1])
        return carry

    lax.fori_loop(0, rows, body, 0, unroll=2)


def _natten(proj, gq, gk, bias, batch, seq):
    rows = seq // GRID_W
    n_pairs = bias.shape[0] // 2
    nwin = WIN_R * GRID_W
    kern = functools.partial(_natten_kernel, rows=rows, scale=D_HEAD ** -0.5)
    return pl.pallas_call(
        kern,
        grid=(batch, n_pairs),
        in_specs=[pl.BlockSpec((seq, LANES), lambda b, p: (b, p)),
                  pl.BlockSpec((seq, LANES), lambda b, p: (b, n_pairs + p)),
                  pl.BlockSpec((seq, LANES), lambda b, p: (b, 2 * n_pairs + p)),
                  pl.BlockSpec((1, LANES), lambda b, p: (0, 0)),
                  pl.BlockSpec((1, LANES), lambda b, p: (0, 0)),
                  pl.BlockSpec((2, WIN_R, GRID_W, nwin), lambda b, p: (p, 0, 0, 0))],
        out_specs=pl.BlockSpec((seq, LANES), lambda b, p: (b, p)),
        out_shape=jax.ShapeDtypeStruct((batch * seq, n_pairs * LANES), F32),
        scratch_shapes=[pltpu.VMEM((2, seq, LANES), BF16),
                        pltpu.VMEM((seq, LANES), BF16),
                        pltpu.VMEM((seq, 2 * LANES), BF16)],
        compiler_params=_params("parallel", "parallel"),
        name="natten",
    )(proj, proj, proj, gq, gk, bias)


def _natten_bias(rpb):
    col = np.arange(GRID_W)
    c0 = np.clip(col - WIN_C // 2, 0, GRID_W - WIN_C)
    ci = np.arange(GRID_W)
    inside = (ci[None, :] >= c0[:, None]) & (ci[None, :] < c0[:, None] + WIN_C)
    cb = ci[None, :] - col[:, None] + (WIN_C - 1)
    onehot = (cb[:, :, None] == np.arange(2 * WIN_C - 1)) & inside[:, :, None]
    toep = jnp.einsum("hrb,cjb->hrcj", rpb.astype(F32), jnp.asarray(onehot, F32),
                      precision=lax.Precision.HIGHEST)
    toep = jnp.where(inside[None, None], toep, NEG_BIAS)
    variants = []
    for v in range(WIN_R):
        rows = toep[:, WIN_R - 1 - v:2 * WIN_R - 1 - v]
        variants.append(rows.transpose(0, 2, 1, 3).reshape(-1, GRID_W, WIN_R * GRID_W))
    return jnp.stack(variants, axis=1)


def _angles(pos, dim, theta):
    inv = jnp.power(theta, -jnp.arange(0, dim, 2, dtype=F32) / dim)
    return pos[:, None] * inv[None, :]


def _rot_tables(parts, seq):
    cos, sin = [], []
    for part in parts:
        if isinstance(part, int):
            cos.append(jnp.ones((seq, part), F32))
            sin.append(jnp.zeros((seq, part), F32))
        else:
            c, s = jnp.cos(part), jnp.sin(part)
            cos += [c, c]
            sin += [-s, s]
    return jnp.concatenate(cos, axis=1), jnp.concatenate(sin, axis=1)


def _dup(v):
    return jnp.concatenate([v, v]).reshape(1, -1).astype(F32)


def _pad_lanes(v):
    return jnp.pad(v.astype(F32), (0, LANES - v.shape[0])).reshape(1, LANES)


def kernel(x, norm_e, w_in_e, gq_a, gk_a, rpb_a, gq_b, gk_b, w_out_e, norm_o, w_in_o, g_cq, w_cq_b, g_ckv, w_ckv_b, gq_c, gk_c, gq_d, gk_d, lam_q1, lam_k1, lam_q2, lam_k2, g_sub_d, w_out_o):
    batch, seq, d = x.shape
    m = batch * seq
    x2 = x.reshape(m, d)
    t = jnp.arange(seq)
    pos = t.astype(F32)
    row = (t // GRID_W).astype(F32)
    col = (t % GRID_W).astype(F32)
    half = D_HEAD // 2
    ones_row = jnp.ones((1, LANES), F32)
    unit_coef = jnp.ones((8, ATTN_TQ), F32)

    n_a = 8 * D_HEAD
    base_b = 3 * n_a
    kb0 = base_b + 8 * D_HEAD
    vb0 = kb0 + 2 * D_HEAD
    gate0 = vb0 + 2 * D_HEAD
    w0 = w_in_e[0]
    dup_heads = [w0[:, c:c + D_HEAD] for c in (kb0, kb0, kb0 + D_HEAD, kb0 + D_HEAD,
                                                 vb0, vb0, vb0 + D_HEAD, vb0 + D_HEAD)]
    w_e = jnp.concatenate([w0[:, :kb0]] + dup_heads + [w0[:, gate0:]],
                          axis=1).astype(BF16)
    proj_e = _proj_in(x2, norm_e[0], w_e)
    bias = _natten_bias(rpb_a[0])
    oa = _natten(proj_e, _dup(gq_a[0]), _dup(gk_a[0]), bias, batch, seq)
    ang_row = _angles(row, half, AXIAL_THETA)
    ang_col = _angles(col, half, AXIAL_THETA)
    cos_b, sin_b = _rot_tables([ang_row, ang_col, ang_row, ang_col], seq)
    kt_b = _kprep(proj_e, 16, 2, batch, seq, _dup(gk_b[0]), cos_b, sin_b,
                  n_groups=2, n_eff=D_HEAD, shift=half // 2)
    vp_b = _vprep(proj_e, 18, 2, batch, seq)
    ob = _attention(proj_e, lambda p: (12 + p, 12 + p), kt_b, lambda p: (p // 2, p // 2),
                    vp_b, lambda p: p // 2, _dup(gq_b[0]), cos_b, sin_b, unit_coef, ones_row,
                    batch=batch, seq=seq, n_pairs=4, n_groups=2, n_eff=D_HEAD,
                    shift=half // 2, scale=D_HEAD ** -0.5, split_q=True, combine="select",
                    post_norm=False)
    x2 = _proj_out(oa, ob, proj_e, 5, w_out_e[0].astype(BF16), x2)

    w1 = w_in_o[0]
    q_lora, kv_lora = g_cq.shape[1], g_ckv.shape[1]
    o_kpe = q_lora + kv_lora
    o_qd = o_kpe + C_ROPE
    zeros = functools.partial(jnp.zeros, dtype=w1.dtype)
    w_o = jnp.concatenate([
        w1[:, :o_kpe],
        zeros((d, C_NOPE)), w1[:, o_kpe:o_qd], zeros((d, LANES - C_NOPE - C_ROPE)),
        w1[:, o_qd:]], axis=1).astype(BF16)
    proj_o = _proj_in(x2, norm_o[0], w_o)
    n_hc = 8
    dqc = C_NOPE + C_ROPE
    wq = jnp.pad(w_cq_b[0].reshape(q_lora, n_hc, dqc),
                 ((0, 0), (0, 0), (0, LANES - dqc))).reshape(q_lora, n_hc * LANES).astype(BF16)
    wkv = w_ckv_b[0].reshape(kv_lora, n_hc, C_NOPE + C_V)
    wk = jnp.pad(wkv[:, :, :C_NOPE],
                 ((0, 0), (0, 0), (0, LANES - C_NOPE))).reshape(kv_lora, n_hc * LANES).astype(BF16)
    wv = wkv[:, :, C_NOPE:].reshape(kv_lora, n_hc * C_V).astype(BF16)
    q_c, k_c, v_c = _mla_up(proj_o, g_cq[0], g_ckv[0], wq, wk, wv)
    ang_mla = _angles(pos, C_ROPE, MLA_THETA)
    cos_c, sin_c = _rot_tables([C_NOPE, ang_mla, LANES - C_NOPE - C_ROPE], seq)
    kt_c = _kprep(k_c, 0, n_hc, batch, seq, _pad_lanes(gk_c[0]), cos_c, sin_c,
                  n_groups=1, n_eff=dqc, shift=C_ROPE // 2)
    vp_c = _vprep(v_c, 0, n_hc // 2, batch, seq)
    oc = _attention(q_c, lambda p: (2 * p, 2 * p + 1), kt_c, lambda p: (2 * p, 2 * p + 1),
                    vp_c, lambda p: p, _pad_lanes(gq_c[0]), cos_c, sin_c, unit_coef, ones_row,
                    batch=batch, seq=seq, n_pairs=4, n_groups=1, n_eff=dqc,
                    shift=C_ROPE // 2, scale=dqc ** -0.5, split_q=False, combine="select",
                    post_norm=False)

    ang_part = _angles(pos, ROT_DIM, ROPE_THETA)
    cos_d, sin_d = _rot_tables([ang_part, D_HEAD - ROT_DIM, ang_part, D_HEAD - ROT_DIM], seq)
    lam_init = 0.8 - 0.6 * math.exp(-0.3 * 1)
    lam = (jnp.exp(jnp.sum(lam_q1[0].astype(F32) * lam_k1[0].astype(F32)))
           - jnp.exp(jnp.sum(lam_q2[0].astype(F32) * lam_k2[0].astype(F32))) + lam_init)
    diff_coef = jnp.full((8, ATTN_TQ), -lam, F32)
    gpost = (g_sub_d[0].astype(F32) * (1.0 - lam_init)).reshape(1, LANES)
    kt_d = _kprep(proj_o, 8, 4, batch, seq, _dup(gk_d[0]), cos_d, sin_d,
                  n_groups=2, n_eff=D_HEAD, shift=ROT_DIM // 2)
    vp_d = _vprep(proj_o, 12, 4, batch, seq)
    od = _attention(proj_o, lambda p: (4 + p, 4 + p), kt_d, lambda p: (p, p),
                    vp_d, lambda p: p, _dup(gq_d[0]), cos_d, sin_d, diff_coef, gpost,
                    batch=batch, seq=seq, n_pairs=4, n_groups=2, n_eff=D_HEAD,
                    shift=ROT_DIM // 2, scale=D_HEAD ** -0.5, split_q=True, combine="add",
                    post_norm=True)
    x2 = _proj_out(oc, od, proj_o, 4, w_out_o[0].astype(BF16), x2)
    return x2.reshape(batch, seq, d)
```

```python
import functools
import math

import jax
import jax.numpy as jnp
import numpy as np
from jax import lax
from jax.experimental import pallas as pl
from jax.experimental.pallas import tpu as pltpu

F32 = jnp.float32
BF16 = jnp.bfloat16

LANES = 128
VMEM_LIMIT_BYTES = 56 * 1024 * 1024

D_HEAD = 64
GRID_W = 64
RMS_EPS = 1e-6
WIN_R = 8
WIN_C = 16
AXIAL_THETA = 10000.0
MLA_THETA = 10000.0
ROPE_THETA = 500000.0
C_NOPE = 64
C_ROPE = 32
C_V = 64
ROT_DIM = 16
NEG_BIAS = -1e30

PROJ_TM = 512
PROJ_TN = 512
ATTN_TQ = 256
ATTN_TK = 1024
FOLD_WAYS = 4
PREP_ROWS = 512
ONES_ROWS = 16
VT_HALF = D_HEAD + ONES_ROWS


def _params(*sem):
    return pltpu.CompilerParams(dimension_semantics=sem,
                                vmem_limit_bytes=VMEM_LIMIT_BYTES)


def _lane_iota(shape):
    return lax.broadcasted_iota(jnp.int32, shape, len(shape) - 1)


def _group_norm(x, gain, n_groups, n_eff):
    y = x * x
    tot = jnp.sum(y, axis=-1, keepdims=True)
    if n_groups == 1:
        ms = tot * (1.0 / n_eff)
    else:
        lo_mask = _lane_iota(x.shape) < D_HEAD
        lo = jnp.sum(jnp.where(lo_mask, y, 0.0), axis=-1, keepdims=True)
        ms = jnp.where(lo_mask, lo, tot - lo) * (1.0 / D_HEAD)
    return x * lax.rsqrt(ms + RMS_EPS) * gain


def _rotate(x, cos, sin_signed, shift):
    n = x.shape[-1]
    up = pltpu.roll(x, n - shift, 1)
    down = pltpu.roll(x, shift, 1)
    partner = jnp.where((_lane_iota(x.shape) & shift) == 0, up, down)
    return x * cos + partner * sin_signed


def _proj_in_kernel(x_ref, g_ref, w_ref, o_ref):
    x = x_ref[...]
    ms = jnp.mean(x * x, axis=-1, keepdims=True)
    h = (x * lax.rsqrt(ms + RMS_EPS) * g_ref[...]).astype(BF16)
    n = o_ref.shape[1]
    for n0 in range(0, n, PROJ_TN):
        o_ref[:, n0:n0 + PROJ_TN] = jnp.dot(
            h, w_ref[:, n0:n0 + PROJ_TN], preferred_element_type=F32)


def _proj_in(x2, gain, w):
    m, d = x2.shape
    n = w.shape[1]
    return pl.pallas_call(
        _proj_in_kernel,
        grid=(m // PROJ_TM,),
        in_specs=[pl.BlockSpec((PROJ_TM, d), lambda i: (i, 0)),
                  pl.BlockSpec((1, d), lambda i: (0, 0)),
                  pl.BlockSpec((d, n), lambda i: (0, 0))],
        out_specs=pl.BlockSpec((PROJ_TM, n), lambda i: (i, 0)),
        out_shape=jax.ShapeDtypeStruct((m, n), F32),
        compiler_params=_params("parallel"),
        name="proj_in",
    )(x2, gain.reshape(1, d), w)


def _proj_out_kernel(oa_ref, ob_ref, ga_ref, gb_ref, w_ref, x_ref, o_ref):
    def gated(o_r, g_r):
        g = g_r[...]
        return (o_r[...] * (g / (1.0 + jnp.exp(-g)))).astype(BF16)

    half = oa_ref.shape[1]
    y = jnp.dot(gated(oa_ref, ga_ref), w_ref[:half, :], preferred_element_type=F32)
    y = y + jnp.dot(gated(ob_ref, gb_ref), w_ref[half:, :], preferred_element_type=F32)
    o_ref[...] = x_ref[...] + y


def _proj_out(oa, ob, proj, gate_blk, w, x2):
    m, d = x2.shape
    half = oa.shape[1]
    return pl.pallas_call(
        _proj_out_kernel,
        grid=(m // PROJ_TM,),
        in_specs=[pl.BlockSpec((PROJ_TM, half), lambda i: (i, 0)),
                  pl.BlockSpec((PROJ_TM, half), lambda i: (i, 0)),
                  pl.BlockSpec((PROJ_TM, half), lambda i: (i, gate_blk)),
                  pl.BlockSpec((PROJ_TM, half), lambda i: (i, gate_blk + 1)),
                  pl.BlockSpec((2 * half, d), lambda i: (0, 0)),
                  pl.BlockSpec((PROJ_TM, d), lambda i: (i, 0))],
        out_specs=pl.BlockSpec((PROJ_TM, d), lambda i: (i, 0)),
        out_shape=jax.ShapeDtypeStruct((m, d), F32),
        compiler_params=_params("parallel"),
        name="proj_out",
    )(oa, ob, proj, proj, w, x2)


def _mla_up_kernel(cq_ref, ckv_ref, kpe_ref, gq_ref, gkv_ref, wq_ref, wk_ref, wv_ref,
                   q_ref, k_ref, v_ref):
    def normed(x, g):
        ms = jnp.mean(x * x, axis=-1, keepdims=True)
        return (x * lax.rsqrt(ms + RMS_EPS) * g).astype(BF16)

    hq = normed(cq_ref[...], gq_ref[...])
    hkv = normed(ckv_ref[...], gkv_ref[...])
    q_ref[...] = jnp.dot(hq, wq_ref[...], preferred_element_type=F32)
    v_ref[...] = jnp.dot(hkv, wv_ref[...], preferred_element_type=F32)
    kk = jnp.dot(hkv, wk_ref[...], preferred_element_type=F32)
    kpe = kpe_ref[...]
    for h in range(k_ref.shape[1] // LANES):
        k_ref[:, h * LANES:(h + 1) * LANES] = kk[:, h * LANES:(h + 1) * LANES] + kpe


def _mla_up(proj, g_cq, g_ckv, wq, wk, wv):
    m = proj.shape[0]
    ql, kvl = wq.shape[0], wk.shape[0]
    tm = PROJ_TM
    return pl.pallas_call(
        _mla_up_kernel,
        grid=(m // tm,),
        in_specs=[pl.BlockSpec((tm, ql), lambda i: (i, 0)),
                  pl.BlockSpec((tm, kvl), lambda i: (i, ql // kvl)),
                  pl.BlockSpec((tm, LANES), lambda i: (i, (ql + kvl) // LANES)),
                  pl.BlockSpec((1, ql), lambda i: (0, 0)),
                  pl.BlockSpec((1, kvl), lambda i: (0, 0)),
                  pl.BlockSpec(wq.shape, lambda i: (0, 0)),
                  pl.BlockSpec(wk.shape, lambda i: (0, 0)),
                  pl.BlockSpec(wv.shape, lambda i: (0, 0))],
        out_specs=[pl.BlockSpec((tm, wq.shape[1]), lambda i: (i, 0)),
                   pl.BlockSpec((tm, wk.shape[1]), lambda i: (i, 0)),
                   pl.BlockSpec((tm, wv.shape[1]), lambda i: (i, 0))],
        out_shape=[jax.ShapeDtypeStruct((m, wq.shape[1]), F32),
                   jax.ShapeDtypeStruct((m, wk.shape[1]), F32),
                   jax.ShapeDtypeStruct((m, wv.shape[1]), F32)],
        compiler_params=_params("parallel"),
        name="mla_up",
    )(proj, proj, proj, g_cq.reshape(1, ql), g_ckv.reshape(1, kvl), wq, wk, wv)


def _kprep_kernel(k_ref, g_ref, cos_ref, sin_ref, kn_ref, *, n_groups, n_eff, shift):
    s = k_ref.shape[0]
    for r0 in range(0, s, PREP_ROWS):
        k = _group_norm(k_ref[r0:r0 + PREP_ROWS, :], g_ref[...], n_groups, n_eff)
        k = _rotate(k, cos_ref[r0:r0 + PREP_ROWS, :], sin_ref[r0:r0 + PREP_ROWS, :], shift)
        kn_ref[0, 0, r0:r0 + PREP_ROWS, :] = k.astype(BF16)


def _kprep(src, blk0, n_units, batch, seq, gain, cos, sin, *, n_groups, n_eff, shift):
    return pl.pallas_call(
        functools.partial(_kprep_kernel, n_groups=n_groups, n_eff=n_eff, shift=shift),
        grid=(batch, n_units),
        in_specs=[pl.BlockSpec((seq, LANES), lambda b, u: (b, blk0 + u)),
                  pl.BlockSpec((1, LANES), lambda b, u: (0, 0)),
                  pl.BlockSpec((seq, LANES), lambda b, u: (0, 0)),
                  pl.BlockSpec((seq, LANES), lambda b, u: (0, 0))],
        out_specs=pl.BlockSpec((1, 1, seq, LANES), lambda b, u: (b, u, 0, 0)),
        out_shape=jax.ShapeDtypeStruct((batch, n_units, seq, LANES), BF16),
        compiler_params=_params("parallel", "parallel"),
        name="kprep",
    )(src, gain, cos, sin)


def _vprep_kernel(v_ref, o_ref):
    s = v_ref.shape[0]
    ones = jnp.ones((ONES_ROWS, PREP_ROWS), BF16)
    for r0 in range(0, s, PREP_ROWS):
        cols = slice(r0, r0 + PREP_ROWS)
        vt = v_ref[cols, :].T.astype(BF16)
        for h in range(2):
            base = h * VT_HALF
            o_ref[0, 0, base:base + D_HEAD, cols] = vt[h * D_HEAD:(h + 1) * D_HEAD]
            o_ref[0, 0, base + D_HEAD:base + VT_HALF, cols] = ones


def _vprep(src, blk0, n_tiles, batch, seq):
    return pl.pallas_call(
        _vprep_kernel,
        grid=(batch, n_tiles),
        in_specs=[pl.BlockSpec((seq, LANES), lambda b, u: (b, blk0 + u))],
        out_specs=pl.BlockSpec((1, 1, 2 * VT_HALF, seq), lambda b, u: (b, u, 0, 0)),
        out_shape=jax.ShapeDtypeStruct((batch, n_tiles, 2 * VT_HALF, seq), BF16),
        compiler_params=_params("parallel", "parallel"),
        name="vprep",
    )(src)


def _attn_kernel(qe_ref, qo_ref, gq_ref, cos_ref, sin_ref, ke_ref, ko_ref, vt_ref,
                 coef_ref, gpost_ref, o_ref, s0_scr, s1_scr, qt_scr, res_scr,
                 *, n_groups, n_eff, shift, qscale, split_q, combine, post_norm):
    seq = vt_ref.shape[3]
    tq = ATTN_TQ
    nq = seq // tq
    rv = res_scr.shape[1]
    q_refs, k_refs = (qe_ref, qo_ref), (ke_ref, ko_ref)
    s_scrs = (s0_scr, s1_scr)
    lane = _lane_iota((tq, LANES))

    def block_rows(i):
        if isinstance(i, int):
            return pl.ds(i * tq, tq)
        return pl.ds(pl.multiple_of(i * tq, tq), tq)

    def fold8(x, op):
        part = op(x.reshape(FOLD_WAYS, x.shape[0] // (8 * FOLD_WAYS), 8, x.shape[1]), axis=1)
        return op(part, axis=0)

    def tree(parts, op):
        while len(parts) > 1:
            parts = [op(a, b) for a, b in zip(parts[::2], parts[1::2])] + parts[len(parts) & ~1:]
        return parts[0]

    def prep_q(i, u):
        rows = block_rows(i)
        q = _group_norm(q_refs[u][rows, :], gq_ref[...], n_groups, n_eff)
        q = _rotate(q, cos_ref[rows, :], sin_ref[rows, :], shift) * qscale
        if split_q:
            q = jnp.where((lane < D_HEAD) == (u == 0), q, 0.0)
        qt_scr[u] = q.T.astype(BF16)

    def tick(scores=None, softmax=None):
        acc = None
        maxes = []
        if scores is not None:
            qt = qt_scr[scores]
        for c0 in range(0, seq, ATTN_TK):
            ks = slice(c0, c0 + ATTN_TK)
            if scores is not None:
                s = jnp.dot(k_refs[scores][0, 0, ks, :], qt, preferred_element_type=F32)
                s_scrs[scores][ks, :] = s
                maxes.append(fold8(s, jnp.max))
            if softmax is not None:
                u, m = softmax
                p = jnp.exp2(s_scrs[u][ks, :] - m).astype(BF16)
                vrows = (slice(u * VT_HALF, (u + 1) * VT_HALF)
                         if combine == "select" else slice(0, VT_HALF + D_HEAD))
                pv = jnp.dot(vt_ref[0, 0, vrows, ks], p, preferred_element_type=F32)
                acc = pv if acc is None else acc + pv
        m_new = jnp.max(tree(maxes, jnp.maximum), axis=0, keepdims=True) if maxes else None
        res = None
        if acc is not None:
            inv_l = 1.0 / acc[D_HEAD:D_HEAD + 1]
            halves = [acc[h:h + D_HEAD] for h in range(0, acc.shape[0], VT_HALF)]
            res = jnp.concatenate(halves, axis=0) * inv_l
        return m_new, res

    def emit(i, res_e, res_o):
        if combine == "select":
            out_t = jnp.concatenate([res_e, res_o], axis=0)
        else:
            out_t = res_e + coef_ref[0:1, :] * res_o
        if post_norm:
            ms = jnp.mean(out_t * out_t, axis=0, keepdims=True)
            out_t = out_t * lax.rsqrt(ms + RMS_EPS)
        o_ref[block_rows(i), :] = out_t.T * gpost_ref[...]

    prep_q(0, 0)
    prep_q(0, 1)
    m_e, _ = tick(scores=0)
    prep_q(min(1, nq - 1), 0)
    m_o, res_e = tick(scores=1, softmax=(0, m_e))
    res_scr[0] = res_e
    prep_q(min(1, nq - 1), 1)

    def body(i, m_o):
        m_e_next, res_o = tick(scores=0, softmax=(1, m_o))
        ahead = jnp.minimum(i + 2, nq - 1)
        prep_q(ahead, 0)
        emit(i, res_scr[0], res_o)
        m_o_next, res_e = tick(scores=1, softmax=(0, m_e_next))
        prep_q(ahead, 1)
        res_scr[0] = res_e
        return m_o_next

    m_o = lax.fori_loop(0, nq - 1, body, m_o)
    _, res_o = tick(softmax=(1, m_o))
    emit(nq - 1, res_scr[0], res_o)


def _attention(q_src, q_blk, kn, k_idx, vt, v_idx, gq, cos, sin, coef, gpost, *,
               batch, seq, n_pairs, n_groups, n_eff, shift, scale, split_q, combine,
               post_norm):
    tq = ATTN_TQ
    m = batch * seq

    def qmap(which):
        return lambda b, p: (b, q_blk(p)[which])

    def kmap(which):
        return lambda b, p: (b, k_idx(p)[which], 0, 0)

    kern = functools.partial(_attn_kernel, n_groups=n_groups, n_eff=n_eff, shift=shift,
                             qscale=scale * math.log2(math.e), split_q=split_q,
                             combine=combine, post_norm=post_norm)
    return pl.pallas_call(
        kern,
        grid=(batch, n_pairs),
        in_specs=[pl.BlockSpec((seq, LANES), qmap(0)),
                  pl.BlockSpec((seq, LANES), qmap(1)),
                  pl.BlockSpec((1, LANES), lambda b, p: (0, 0)),
                  pl.BlockSpec((seq, LANES), lambda b, p: (0, 0),
                               pipeline_mode=pl.Buffered(1)),
                  pl.BlockSpec((seq, LANES), lambda b, p: (0, 0),
                               pipeline_mode=pl.Buffered(1)),
                  pl.BlockSpec((1, 1, seq, LANES), kmap(0)),
                  pl.BlockSpec((1, 1, seq, LANES), kmap(1)),
                  pl.BlockSpec((1, 1, 2 * VT_HALF, seq), lambda b, p: (b, v_idx(p), 0, 0)),
                  pl.BlockSpec((8, tq), lambda b, p: (0, 0)),
                  pl.BlockSpec((1, LANES), lambda b, p: (0, 0))],
        out_specs=pl.BlockSpec((seq, LANES), lambda b, p: (b, p)),
        out_shape=jax.ShapeDtypeStruct((m, n_pairs * LANES), F32),
        scratch_shapes=[pltpu.VMEM((seq, tq), F32), pltpu.VMEM((seq, tq), F32),
                        pltpu.VMEM((2, LANES, tq), BF16),
                        pltpu.VMEM((1, D_HEAD if combine == "select" else LANES, tq), F32)],
        compiler_params=_params("parallel", "parallel"),
        name="attention",
    )(q_src, q_src, gq, cos, sin, kn, kn, vt, coef, gpost)


def _natten_kernel(q_ref, k_ref, v_ref, gq_ref, gk_ref, bias_ref, o_ref,
                   q_scr, k_scr, v_scr, *, rows, scale):
    seq = q_ref.shape[0]
    nwin = WIN_R * GRID_W
    for r0 in range(0, seq, PREP_ROWS):
        sl = slice(r0, r0 + PREP_ROWS)
        q = _group_norm(q_ref[sl, :], gq_ref[...], 2, D_HEAD) * scale
        lo = _lane_iota(q.shape) < D_HEAD
        q_scr[0, sl, :] = jnp.where(lo, q, 0.0).astype(BF16)
        q_scr[1, sl, :] = jnp.where(lo, 0.0, q).astype(BF16)
        k_scr[sl, :] = _group_norm(k_ref[sl, :], gk_ref[...], 2, D_HEAD).astype(BF16)
        v_scr[sl, :LANES] = v_ref[sl, :].astype(BF16)
        v_scr[sl, LANES:] = jnp.ones((PREP_ROWS, LANES), BF16)

    lane_lo = _lane_iota((GRID_W, LANES)) < D_HEAD

    def body(r, carry):
        w0 = jnp.clip(r - WIN_R // 2, 0, rows - WIN_R)
        var = r - w0
        kbase = pl.multiple_of(w0 * GRID_W, GRID_W)
        qbase = pl.multiple_of(r * GRID_W, GRID_W)
        kw = k_scr[pl.ds(kbase, nwin), :]
        vw = v_scr[pl.ds(kbase, nwin), :]
        qrows = pl.ds(qbase, GRID_W)
        q2 = jnp.concatenate([q_scr[0, qrows, :], q_scr[1, qrows, :]], axis=0)
        s = lax.dot_general(q2, kw, (((1,), (1,)), ((), ())),
                            preferred_element_type=F32)
        z = s + bias_ref[0, var]
        zmax = jnp.max(z, axis=-1, keepdims=True)
        p = jnp.exp2(z - zmax).astype(BF16)
        acc = jnp.dot(p, vw, preferred_element_type=F32)
        res = acc[:, :LANES] / acc[:, LANES:]
        o_ref[qrows, :] = jnp.where(lane_lo, res[:GRID_W], res[GRID_W:])
        return carry

    lax.fori_loop(0, rows, body, 0, unroll=16)


def _natten(proj, gq, gk, bias, batch, seq):
    rows = seq // GRID_W
    n_pairs = bias.shape[0]
    nwin = WIN_R * GRID_W
    kern = functools.partial(_natten_kernel, rows=rows,
                             scale=D_HEAD ** -0.5 * math.log2(math.e))
    return pl.pallas_call(
        kern,
        grid=(batch, n_pairs),
        in_specs=[pl.BlockSpec((seq, LANES), lambda b, p: (b, p)),
                  pl.BlockSpec((seq, LANES), lambda b, p: (b, n_pairs + p)),
                  pl.BlockSpec((seq, LANES), lambda b, p: (b, 2 * n_pairs + p)),
                  pl.BlockSpec((1, LANES), lambda b, p: (0, 0)),
                  pl.BlockSpec((1, LANES), lambda b, p: (0, 0)),
                  pl.BlockSpec((1, WIN_R, 2 * GRID_W, nwin), lambda b, p: (p, 0, 0, 0))],
        out_specs=pl.BlockSpec((seq, LANES), lambda b, p: (b, p)),
        out_shape=jax.ShapeDtypeStruct((batch * seq, n_pairs * LANES), F32),
        scratch_shapes=[pltpu.VMEM((2, seq, LANES), BF16),
                        pltpu.VMEM((seq, LANES), BF16),
                        pltpu.VMEM((seq, 2 * LANES), BF16)],
        compiler_params=_params("parallel", "parallel"),
        name="natten",
    )(proj, proj, proj, gq, gk, bias)


def _natten_bias(rpb):
    col = np.arange(GRID_W)
    c0 = np.clip(col - WIN_C // 2, 0, GRID_W - WIN_C)
    ci = np.arange(GRID_W)
    inside = (ci[None, :] >= c0[:, None]) & (ci[None, :] < c0[:, None] + WIN_C)
    cb = ci[None, :] - col[:, None] + (WIN_C - 1)
    onehot = (cb[:, :, None] == np.arange(2 * WIN_C - 1)) & inside[:, :, None]
    toep = jnp.einsum("hrb,cjb->hrcj", rpb.astype(F32), jnp.asarray(onehot, F32),
                      precision=lax.Precision.HIGHEST)
    toep = jnp.where(inside[None, None], toep * math.log2(math.e), NEG_BIAS)
    n_heads = rpb.shape[0]
    variants = []
    for v in range(WIN_R):
        rows = toep[:, WIN_R - 1 - v:2 * WIN_R - 1 - v]
        variants.append(rows.transpose(0, 2, 1, 3).reshape(n_heads // 2, 2 * GRID_W,
                                                           WIN_R * GRID_W))
    return jnp.stack(variants, axis=1)


def _angles(pos, dim, theta):
    inv = jnp.power(theta, -jnp.arange(0, dim, 2, dtype=F32) / dim)
    return pos[:, None] * inv[None, :]


def _rot_tables(parts, seq):
    cos, sin = [], []
    for part in parts:
        if isinstance(part, int):
            cos.append(jnp.ones((seq, part), F32))
            sin.append(jnp.zeros((seq, part), F32))
        else:
            c, s = jnp.cos(part), jnp.sin(part)
            cos += [c, c]
            sin += [-s, s]
    return jnp.concatenate(cos, axis=1), jnp.concatenate(sin, axis=1)


def _dup(v):
    return jnp.concatenate([v, v]).reshape(1, -1).astype(F32)


def _pad_lanes(v):
    return jnp.pad(v.astype(F32), (0, LANES - v.shape[0])).reshape(1, LANES)


def kernel(x, norm_e, w_in_e, gq_a, gk_a, rpb_a, gq_b, gk_b, w_out_e, norm_o, w_in_o, g_cq, w_cq_b, g_ckv, w_ckv_b, gq_c, gk_c, gq_d, gk_d, lam_q1, lam_k1, lam_q2, lam_k2, g_sub_d, w_out_o):
    batch, seq, d = x.shape
    m = batch * seq
    x2 = x.reshape(m, d)
    t = jnp.arange(seq)
    pos = t.astype(F32)
    row = (t // GRID_W).astype(F32)
    col = (t % GRID_W).astype(F32)
    half = D_HEAD // 2
    ones_row = jnp.ones((1, LANES), F32)
    unit_coef = jnp.ones((8, ATTN_TQ), F32)

    n_a = 8 * D_HEAD
    base_b = 3 * n_a
    kb0 = base_b + 8 * D_HEAD
    vb0 = kb0 + 2 * D_HEAD
    gate0 = vb0 + 2 * D_HEAD
    w0 = w_in_e[0]
    dup_heads = [w0[:, c:c + D_HEAD] for c in (kb0, kb0, kb0 + D_HEAD, kb0 + D_HEAD,
                                                 vb0, vb0, vb0 + D_HEAD, vb0 + D_HEAD)]
    w_e = jnp.concatenate([w0[:, :kb0]] + dup_heads + [w0[:, gate0:]],
                          axis=1).astype(BF16)
    proj_e = _proj_in(x2, norm_e[0], w_e)
    bias = _natten_bias(rpb_a[0])
    oa = _natten(proj_e, _dup(gq_a[0]), _dup(gk_a[0]), bias, batch, seq)
    ang_row = _angles(row, half, AXIAL_THETA)
    ang_col = _angles(col, half, AXIAL_THETA)
    cos_b, sin_b = _rot_tables([ang_row, ang_col, ang_row, ang_col], seq)
    kt_b = _kprep(proj_e, 16, 2, batch, seq, _dup(gk_b[0]), cos_b, sin_b,
                  n_groups=2, n_eff=D_HEAD, shift=half // 2)
    vp_b = _vprep(proj_e, 18, 2, batch, seq)
    ob = _attention(proj_e, lambda p: (12 + p, 12 + p), kt_b, lambda p: (p // 2, p // 2),
                    vp_b, lambda p: p // 2, _dup(gq_b[0]), cos_b, sin_b, unit_coef, ones_row,
                    batch=batch, seq=seq, n_pairs=4, n_groups=2, n_eff=D_HEAD,
                    shift=half // 2, scale=D_HEAD ** -0.5, split_q=True, combine="select",
                    post_norm=False)
    x2 = _proj_out(oa, ob, proj_e, 5, w_out_e[0].astype(BF16), x2)

    w1 = w_in_o[0]
    q_lora, kv_lora = g_cq.shape[1], g_ckv.shape[1]
    o_kpe = q_lora + kv_lora
    o_qd = o_kpe + C_ROPE
    zeros = functools.partial(jnp.zeros, dtype=w1.dtype)
    w_o = jnp.concatenate([
        w1[:, :o_kpe],
        zeros((d, C_NOPE)), w1[:, o_kpe:o_qd], zeros((d, LANES - C_NOPE - C_ROPE)),
        w1[:, o_qd:]], axis=1).astype(BF16)
    proj_o = _proj_in(x2, norm_o[0], w_o)
    n_hc = 8
    dqc = C_NOPE + C_ROPE
    wq = jnp.pad(w_cq_b[0].reshape(q_lora, n_hc, dqc),
                 ((0, 0), (0, 0), (0, LANES - dqc))).reshape(q_lora, n_hc * LANES).astype(BF16)
    wkv = w_ckv_b[0].reshape(kv_lora, n_hc, C_NOPE + C_V)
    wk = jnp.pad(wkv[:, :, :C_NOPE],
                 ((0, 0), (0, 0), (0, LANES - C_NOPE))).reshape(kv_lora, n_hc * LANES).astype(BF16)
    wv = wkv[:, :, C_NOPE:].reshape(kv_lora, n_hc * C_V).astype(BF16)
    q_c, k_c, v_c = _mla_up(proj_o, g_cq[0], g_ckv[0], wq, wk, wv)
    ang_mla = _angles(pos, C_ROPE, MLA_THETA)
    cos_c, sin_c = _rot_tables([C_NOPE, ang_mla, LANES - C_NOPE - C_ROPE], seq)
    kt_c = _kprep(k_c, 0, n_hc, batch, seq, _pad_lanes(gk_c[0]), cos_c, sin_c,
                  n_groups=1, n_eff=dqc, shift=C_ROPE // 2)
    vp_c = _vprep(v_c, 0, n_hc // 2, batch, seq)
    oc = _attention(q_c, lambda p: (2 * p, 2 * p + 1), kt_c, lambda p: (2 * p, 2 * p + 1),
                    vp_c, lambda p: p, _pad_lanes(gq_c[0]), cos_c, sin_c, unit_coef, ones_row,
                    batch=batch, seq=seq, n_pairs=4, n_groups=1, n_eff=dqc,
                    shift=C_ROPE // 2, scale=dqc ** -0.5, split_q=False, combine="select",
                    post_norm=False)

    ang_part = _angles(pos, ROT_DIM, ROPE_THETA)
    cos_d, sin_d = _rot_tables([ang_part, D_HEAD - ROT_DIM, ang_part, D_HEAD - ROT_DIM], seq)
    lam_init = 0.8 - 0.6 * math.exp(-0.3 * 1)
    lam = (jnp.exp(jnp.sum(lam_q1[0].astype(F32) * lam_k1[0].astype(F32)))
           - jnp.exp(jnp.sum(lam_q2[0].astype(F32) * lam_k2[0].astype(F32))) + lam_init)
    diff_coef = jnp.full((8, ATTN_TQ), -lam, F32)
    gpost = (g_sub_d[0].astype(F32) * (1.0 - lam_init)).reshape(1, LANES)
    kt_d = _kprep(proj_o, 8, 4, batch, seq, _dup(gk_d[0]), cos_d, sin_d,
                  n_groups=2, n_eff=D_HEAD, shift=ROT_DIM // 2)
    vp_d = _vprep(proj_o, 12, 4, batch, seq)
    od = _attention(proj_o, lambda p: (4 + p, 4 + p), kt_d, lambda p: (p, p),
                    vp_d, lambda p: p, _dup(gq_d[0]), cos_d, sin_d, diff_coef, gpost,
                    batch=batch, seq=seq, n_pairs=4, n_groups=2, n_eff=D_HEAD,
                    shift=ROT_DIM // 2, scale=D_HEAD ** -0.5, split_q=True, combine="add",
                    post_norm=True)
    x2 = _proj_out(oc, od, proj_o, 4, w_out_o[0].astype(BF16), x2)
    return x2.reshape(batch, seq, d)
```

```python
import functools
import math

import jax
import jax.numpy as jnp
import numpy as np
from jax import lax
from jax.experimental import pallas as pl
from jax.experimental.pallas import tpu as pltpu

F32 = jnp.float32
BF16 = jnp.bfloat16

LANES = 128
VMEM_LIMIT_BYTES = 56 * 1024 * 1024

D_HEAD = 64
GRID_W = 64
RMS_EPS = 1e-6
WIN_R = 8
WIN_C = 16
AXIAL_THETA = 10000.0
MLA_THETA = 10000.0
ROPE_THETA = 500000.0
C_NOPE = 64
C_ROPE = 32
C_V = 64
ROT_DIM = 16
NEG_BIAS = -1e30

PROJ_TM = 512
PROJ_TN = 512
ATTN_TQ = 256
ATTN_TK = 1024
FOLD_WAYS = 4
PREP_ROWS = 512
ONES_ROWS = 16
VT_HALF = D_HEAD + ONES_ROWS


def _params(*sem):
    return pltpu.CompilerParams(dimension_semantics=sem,
                                vmem_limit_bytes=VMEM_LIMIT_BYTES)


def _lane_iota(shape):
    return lax.broadcasted_iota(jnp.int32, shape, len(shape) - 1)


def _group_norm(x, gain, n_groups, n_eff):
    y = x * x
    tot = jnp.sum(y, axis=-1, keepdims=True)
    if n_groups == 1:
        ms = tot * (1.0 / n_eff)
    else:
        lo_mask = _lane_iota(x.shape) < D_HEAD
        lo = jnp.sum(jnp.where(lo_mask, y, 0.0), axis=-1, keepdims=True)
        ms = jnp.where(lo_mask, lo, tot - lo) * (1.0 / D_HEAD)
    return x * lax.rsqrt(ms + RMS_EPS) * gain


def _rotate(x, cos, sin_signed, shift):
    n = x.shape[-1]
    up = pltpu.roll(x, n - shift, 1)
    down = pltpu.roll(x, shift, 1)
    partner = jnp.where((_lane_iota(x.shape) & shift) == 0, up, down)
    return x * cos + partner * sin_signed


def _store_keys(kn_ref, j, tile, gain, cos, sin, *, n_groups, n_eff, shift):
    k = _group_norm(tile, gain, n_groups, n_eff)
    kn_ref[0, j] = _rotate(k, cos, sin, shift).astype(BF16)


def _store_values_t(vt_ref, j, tile):
    vt = tile.T.astype(BF16)
    ones = jnp.ones((ONES_ROWS, vt.shape[1]), BF16)
    for h in range(2):
        base = h * VT_HALF
        vt_ref[0, j, base:base + D_HEAD, :] = vt[h * D_HEAD:(h + 1) * D_HEAD]
        vt_ref[0, j, base + D_HEAD:base + VT_HALF, :] = ones


def _proj_in_kernel(x_ref, g_ref, w_ref, gk_ref, cos_ref, sin_ref, o_ref, kn_ref, vt_ref, *,
                    k_tiles, v_tiles, n_groups, n_eff, shift):
    x = x_ref[...]
    ms = jnp.mean(x * x, axis=-1, keepdims=True)
    h = (x * lax.rsqrt(ms + RMS_EPS) * g_ref[...]).astype(BF16)
    n = o_ref.shape[1]
    for n0 in range(0, n, PROJ_TN):
        o_ref[:, n0:n0 + PROJ_TN] = jnp.dot(
            h, w_ref[:, n0:n0 + PROJ_TN], preferred_element_type=F32)
        for j, t in enumerate(k_tiles):
            if n0 <= t * LANES < n0 + PROJ_TN:
                _store_keys(kn_ref, j, o_ref[:, t * LANES:(t + 1) * LANES], gk_ref[...],
                            cos_ref[...], sin_ref[...], n_groups=n_groups, n_eff=n_eff,
                            shift=shift)
        for j, t in enumerate(v_tiles):
            if n0 <= t * LANES < n0 + PROJ_TN:
                _store_values_t(vt_ref, j, o_ref[:, t * LANES:(t + 1) * LANES])


def _proj_in(x2, gain, w, gk, cos, sin, *, batch, seq, k_tiles, v_tiles, n_groups, n_eff,
             shift):
    m, d = x2.shape
    n = w.shape[1]
    tm = PROJ_TM
    nblk = seq // tm
    nk, nv = len(k_tiles), len(v_tiles)
    kern = functools.partial(_proj_in_kernel, k_tiles=k_tiles, v_tiles=v_tiles,
                             n_groups=n_groups, n_eff=n_eff, shift=shift)
    return pl.pallas_call(
        kern,
        grid=(m // tm,),
        in_specs=[pl.BlockSpec((tm, d), lambda i: (i, 0)),
                  pl.BlockSpec((1, d), lambda i: (0, 0)),
                  pl.BlockSpec((d, n), lambda i: (0, 0)),
                  pl.BlockSpec((1, LANES), lambda i: (0, 0)),
                  pl.BlockSpec((tm, LANES), lambda i: (i % nblk, 0)),
                  pl.BlockSpec((tm, LANES), lambda i: (i % nblk, 0))],
        out_specs=[pl.BlockSpec((tm, n), lambda i: (i, 0)),
                   pl.BlockSpec((1, nk, tm, LANES), lambda i: (i // nblk, 0, i % nblk, 0)),
                   pl.BlockSpec((1, nv, 2 * VT_HALF, tm),
                                lambda i: (i // nblk, 0, 0, i % nblk))],
        out_shape=[jax.ShapeDtypeStruct((m, n), F32),
                   jax.ShapeDtypeStruct((batch, nk, seq, LANES), BF16),
                   jax.ShapeDtypeStruct((batch, nv, 2 * VT_HALF, seq), BF16)],
        compiler_params=_params("parallel"),
        name="proj_in",
    )(x2, gain.reshape(1, d), w, gk, cos, sin)


def _proj_out_kernel(oa_ref, ob_ref, ga_ref, gb_ref, w_ref, x_ref, o_ref):
    def gated(o_r, g_r):
        g = g_r[...]
        return (o_r[...] * (g / (1.0 + jnp.exp(-g)))).astype(BF16)

    half = oa_ref.shape[1]
    y = jnp.dot(gated(oa_ref, ga_ref), w_ref[:half, :], preferred_element_type=F32)
    y = y + jnp.dot(gated(ob_ref, gb_ref), w_ref[half:, :], preferred_element_type=F32)
    o_ref[...] = x_ref[...] + y


def _proj_out(oa, ob, proj, gate_blk, w, x2):
    m, d = x2.shape
    half = oa.shape[1]
    return pl.pallas_call(
        _proj_out_kernel,
        grid=(m // PROJ_TM,),
        in_specs=[pl.BlockSpec((PROJ_TM, half), lambda i: (i, 0)),
                  pl.BlockSpec((PROJ_TM, half), lambda i: (i, 0)),
                  pl.BlockSpec((PROJ_TM, half), lambda i: (i, gate_blk)),
                  pl.BlockSpec((PROJ_TM, half), lambda i: (i, gate_blk + 1)),
                  pl.BlockSpec((2 * half, d), lambda i: (0, 0)),
                  pl.BlockSpec((PROJ_TM, d), lambda i: (i, 0))],
        out_specs=pl.BlockSpec((PROJ_TM, d), lambda i: (i, 0)),
        out_shape=jax.ShapeDtypeStruct((m, d), F32),
        compiler_params=_params("parallel"),
        name="proj_out",
    )(oa, ob, proj, proj, w, x2)


def _mla_up_kernel(cq_ref, ckv_ref, kpe_ref, gq_ref, gkv_ref, wq_ref, wk_ref, wv_ref,
                   gk_ref, cos_ref, sin_ref, q_ref, kn_ref, vt_ref, *, n_eff, shift):
    def normed(x, g):
        ms = jnp.mean(x * x, axis=-1, keepdims=True)
        return (x * lax.rsqrt(ms + RMS_EPS) * g).astype(BF16)

    hq = normed(cq_ref[...], gq_ref[...])
    hkv = normed(ckv_ref[...], gkv_ref[...])
    q_ref[...] = jnp.dot(hq, wq_ref[...], preferred_element_type=F32)
    kk = jnp.dot(hkv, wk_ref[...], preferred_element_type=F32)
    kpe = kpe_ref[...]
    for h in range(kn_ref.shape[1]):
        _store_keys(kn_ref, h, kk[:, h * LANES:(h + 1) * LANES] + kpe, gk_ref[...],
                    cos_ref[...], sin_ref[...], n_groups=1, n_eff=n_eff, shift=shift)
    vv = jnp.dot(hkv, wv_ref[...], preferred_element_type=F32)
    for j in range(vt_ref.shape[1]):
        _store_values_t(vt_ref, j, vv[:, j * LANES:(j + 1) * LANES])


def _mla_up(proj, g_cq, g_ckv, wq, wk, wv, gk, cos, sin, *, batch, seq, n_eff, shift):
    m = proj.shape[0]
    ql, kvl = wq.shape[0], wk.shape[0]
    tm = PROJ_TM
    nblk = seq // tm
    nk, nv = wk.shape[1] // LANES, wv.shape[1] // LANES
    return pl.pallas_call(
        functools.partial(_mla_up_kernel, n_eff=n_eff, shift=shift),
        grid=(m // tm,),
        in_specs=[pl.BlockSpec((tm, ql), lambda i: (i, 0)),
                  pl.BlockSpec((tm, kvl), lambda i: (i, ql // kvl)),
                  pl.BlockSpec((tm, LANES), lambda i: (i, (ql + kvl) // LANES)),
                  pl.BlockSpec((1, ql), lambda i: (0, 0)),
                  pl.BlockSpec((1, kvl), lambda i: (0, 0)),
                  pl.BlockSpec(wq.shape, lambda i: (0, 0)),
                  pl.BlockSpec(wk.shape, lambda i: (0, 0)),
                  pl.BlockSpec(wv.shape, lambda i: (0, 0)),
                  pl.BlockSpec((1, LANES), lambda i: (0, 0)),
                  pl.BlockSpec((tm, LANES), lambda i: (i % nblk, 0)),
                  pl.BlockSpec((tm, LANES), lambda i: (i % nblk, 0))],
        out_specs=[pl.BlockSpec((tm, wq.shape[1]), lambda i: (i, 0)),
                   pl.BlockSpec((1, nk, tm, LANES), lambda i: (i // nblk, 0, i % nblk, 0)),
                   pl.BlockSpec((1, nv, 2 * VT_HALF, tm),
                                lambda i: (i // nblk, 0, 0, i % nblk))],
        out_shape=[jax.ShapeDtypeStruct((m, wq.shape[1]), F32),
                   jax.ShapeDtypeStruct((batch, nk, seq, LANES), BF16),
                   jax.ShapeDtypeStruct((batch, nv, 2 * VT_HALF, seq), BF16)],
        compiler_params=_params("parallel"),
        name="mla_up",
    )(proj, proj, proj, g_cq.reshape(1, ql), g_ckv.reshape(1, kvl), wq, wk, wv, gk, cos, sin)


def _attn_kernel(qe_ref, qo_ref, gq_ref, cos_ref, sin_ref, ke_ref, ko_ref, vt_ref,
                 coef_ref, gpost_ref, o_ref, s0_scr, s1_scr, qt_scr, res_scr,
                 *, n_groups, n_eff, shift, qscale, split_q, combine, post_norm):
    seq = vt_ref.shape[3]
    tq = ATTN_TQ
    nq = seq // tq
    rv = res_scr.shape[1]
    q_refs, k_refs = (qe_ref, qo_ref), (ke_ref, ko_ref)
    s_scrs = (s0_scr, s1_scr)
    lane = _lane_iota((tq, LANES))

    def block_rows(i):
        if isinstance(i, int):
            return pl.ds(i * tq, tq)
        return pl.ds(pl.multiple_of(i * tq, tq), tq)

    def fold8(x, op):
        part = op(x.reshape(FOLD_WAYS, x.shape[0] // (8 * FOLD_WAYS), 8, x.shape[1]), axis=1)
        return op(part, axis=0)

    def tree(parts, op):
        while len(parts) > 1:
            parts = [op(a, b) for a, b in zip(parts[::2], parts[1::2])] + parts[len(parts) & ~1:]
        return parts[0]

    def prep_q(i, u):
        rows = block_rows(i)
        q = _group_norm(q_refs[u][rows, :], gq_ref[...], n_groups, n_eff)
        q = _rotate(q, cos_ref[rows, :], sin_ref[rows, :], shift) * qscale
        if split_q:
            q = jnp.where((lane < D_HEAD) == (u == 0), q, 0.0)
        qt_scr[u] = q.T.astype(BF16)

    def tick(scores=None, softmax=None):
        acc = None
        maxes = []
        if scores is not None:
            qt = qt_scr[scores]
        for c0 in range(0, seq, ATTN_TK):
            ks = slice(c0, c0 + ATTN_TK)
            if scores is not None:
                s = jnp.dot(k_refs[scores][0, 0, ks, :], qt, preferred_element_type=F32)
                s_scrs[scores][ks, :] = s
                maxes.append(fold8(s, jnp.max))
            if softmax is not None:
                u, m = softmax
                p = jnp.exp2(s_scrs[u][ks, :] - m).astype(BF16)
                vrows = (slice(u * VT_HALF, (u + 1) * VT_HALF)
                         if combine == "select" else slice(0, VT_HALF + D_HEAD))
                pv = jnp.dot(vt_ref[0, 0, vrows, ks], p, preferred_element_type=F32)
                acc = pv if acc is None else acc + pv
        m_new = jnp.max(tree(maxes, jnp.maximum), axis=0, keepdims=True) if maxes else None
        res = None
        if acc is not None:
            inv_l = 1.0 / acc[D_HEAD:D_HEAD + 1]
            halves = [acc[h:h + D_HEAD] for h in range(0, acc.shape[0], VT_HALF)]
            res = jnp.concatenate(halves, axis=0) * inv_l
        return m_new, res

    def emit(i, res_e, res_o):
        if combine == "select":
            out_t = jnp.concatenate([res_e, res_o], axis=0)
        else:
            out_t = res_e + coef_ref[0:1, :] * res_o
        if post_norm:
            ms = jnp.mean(out_t * out_t, axis=0, keepdims=True)
            out_t = out_t * lax.rsqrt(ms + RMS_EPS)
        o_ref[block_rows(i), :] = out_t.T * gpost_ref[...]

    prep_q(0, 0)
    prep_q(0, 1)
    m_e, _ = tick(scores=0)
    prep_q(min(1, nq - 1), 0)
    m_o, res_e = tick(scores=1, softmax=(0, m_e))
    res_scr[0] = res_e
    prep_q(min(1, nq - 1), 1)

    def body(i, m_o):
        m_e_next, res_o = tick(scores=0, softmax=(1, m_o))
        ahead = jnp.minimum(i + 2, nq - 1)
        prep_q(ahead, 0)
        emit(i, res_scr[0], res_o)
        m_o_next, res_e = tick(scores=1, softmax=(0, m_e_next))
        prep_q(ahead, 1)
        res_scr[0] = res_e
        return m_o_next

    m_o = lax.fori_loop(0, nq - 1, body, m_o)
    _, res_o = tick(softmax=(1, m_o))
    emit(nq - 1, res_scr[0], res_o)


def _attention(q_src, q_blk, kn, k_idx, vt, v_idx, gq, cos, sin, coef, gpost, *,
               batch, seq, n_pairs, n_groups, n_eff, shift, scale, split_q, combine,
               post_norm):
    tq = ATTN_TQ
    m = batch * seq

    def qmap(which):
        return lambda b, p: (b, q_blk(p)[which])

    def kmap(which):
        return lambda b, p: (b, k_idx(p)[which], 0, 0)

    kern = functools.partial(_attn_kernel, n_groups=n_groups, n_eff=n_eff, shift=shift,
                             qscale=scale * math.log2(math.e), split_q=split_q,
                             combine=combine, post_norm=post_norm)
    return pl.pallas_call(
        kern,
        grid=(batch, n_pairs),
        in_specs=[pl.BlockSpec((seq, LANES), qmap(0)),
                  pl.BlockSpec((seq, LANES), qmap(1)),
                  pl.BlockSpec((1, LANES), lambda b, p: (0, 0)),
                  pl.BlockSpec((seq, LANES), lambda b, p: (0, 0),
                               pipeline_mode=pl.Buffered(1)),
                  pl.BlockSpec((seq, LANES), lambda b, p: (0, 0),
                               pipeline_mode=pl.Buffered(1)),
                  pl.BlockSpec((1, 1, seq, LANES), kmap(0)),
                  pl.BlockSpec((1, 1, seq, LANES), kmap(1)),
                  pl.BlockSpec((1, 1, 2 * VT_HALF, seq), lambda b, p: (b, v_idx(p), 0, 0)),
                  pl.BlockSpec((8, tq), lambda b, p: (0, 0)),
                  pl.BlockSpec((1, LANES), lambda b, p: (0, 0))],
        out_specs=pl.BlockSpec((seq, LANES), lambda b, p: (b, p)),
        out_shape=jax.ShapeDtypeStruct((m, n_pairs * LANES), F32),
        scratch_shapes=[pltpu.VMEM((seq, tq), F32), pltpu.VMEM((seq, tq), F32),
                        pltpu.VMEM((2, LANES, tq), BF16),
                        pltpu.VMEM((1, D_HEAD if combine == "select" else LANES, tq), F32)],
        compiler_params=_params("parallel", "parallel"),
        name="attention",
    )(q_src, q_src, gq, cos, sin, kn, kn, vt, coef, gpost)


def _natten_kernel(q_ref, k_ref, v_ref, gq_ref, gk_ref, bias_ref, o_ref,
                   q_scr, k_scr, v_scr, *, rows, scale):
    seq = q_ref.shape[0]
    nwin = WIN_R * GRID_W
    for r0 in range(0, seq, PREP_ROWS):
        sl = slice(r0, r0 + PREP_ROWS)
        q = _group_norm(q_ref[sl, :], gq_ref[...], 2, D_HEAD) * scale
        lo = _lane_iota(q.shape) < D_HEAD
        q_scr[0, sl, :] = jnp.where(lo, q, 0.0).astype(BF16)
        q_scr[1, sl, :] = jnp.where(lo, 0.0, q).astype(BF16)
        k_scr[sl, :] = _group_norm(k_ref[sl, :], gk_ref[...], 2, D_HEAD).astype(BF16)
        v_scr[sl, :LANES] = v_ref[sl, :].astype(BF16)
        v_scr[sl, LANES:] = jnp.ones((PREP_ROWS, LANES), BF16)

    lane_lo = _lane_iota((GRID_W, LANES)) < D_HEAD

    def body(r, carry):
        w0 = jnp.clip(r - WIN_R // 2, 0, rows - WIN_R)
        var = r - w0
        kbase = pl.multiple_of(w0 * GRID_W, GRID_W)
        qbase = pl.multiple_of(r * GRID_W, GRID_W)
        kw = k_scr[pl.ds(kbase, nwin), :]
        vw = v_scr[pl.ds(kbase, nwin), :]
        qrows = pl.ds(qbase, GRID_W)
        q2 = jnp.concatenate([q_scr[0, qrows, :], q_scr[1, qrows, :]], axis=0)
        s = lax.dot_general(q2, kw, (((1,), (1,)), ((), ())),
                            preferred_element_type=F32)
        z = s + bias_ref[0, var]
        zmax = jnp.max(z, axis=-1, keepdims=True)
        p = jnp.exp2(z - zmax).astype(BF16)
        acc = jnp.dot(p, vw, preferred_element_type=F32)
        res = acc[:, :LANES] / acc[:, LANES:]
        o_ref[qrows, :] = jnp.where(lane_lo, res[:GRID_W], res[GRID_W:])
        return carry

    lax.fori_loop(0, rows, body, 0, unroll=16)


def _natten(proj, gq, gk, bias, batch, seq):
    rows = seq // GRID_W
    n_pairs = bias.shape[0]
    nwin = WIN_R * GRID_W
    kern = functools.partial(_natten_kernel, rows=rows,
                             scale=D_HEAD ** -0.5 * math.log2(math.e))
    return pl.pallas_call(
        kern,
        grid=(batch, n_pairs),
        in_specs=[pl.BlockSpec((seq, LANES), lambda b, p: (b, p)),
                  pl.BlockSpec((seq, LANES), lambda b, p: (b, n_pairs + p)),
                  pl.BlockSpec((seq, LANES), lambda b, p: (b, 2 * n_pairs + p)),
                  pl.BlockSpec((1, LANES), lambda b, p: (0, 0)),
                  pl.BlockSpec((1, LANES), lambda b, p: (0, 0)),
                  pl.BlockSpec((1, WIN_R, 2 * GRID_W, nwin), lambda b, p: (p, 0, 0, 0))],
        out_specs=pl.BlockSpec((seq, LANES), lambda b, p: (b, p)),
        out_shape=jax.ShapeDtypeStruct((batch * seq, n_pairs * LANES), F32),
        scratch_shapes=[pltpu.VMEM((2, seq, LANES), BF16),
                        pltpu.VMEM((seq, LANES), BF16),
                        pltpu.VMEM((seq, 2 * LANES), BF16)],
        compiler_params=_params("parallel", "parallel"),
        name="natten",
    )(proj, proj, proj, gq, gk, bias)


def _natten_bias(rpb):
    col = np.arange(GRID_W)
    c0 = np.clip(col - WIN_C // 2, 0, GRID_W - WIN_C)
    ci = np.arange(GRID_W)
    inside = (ci[None, :] >= c0[:, None]) & (ci[None, :] < c0[:, None] + WIN_C)
    cb = ci[None, :] - col[:, None] + (WIN_C - 1)
    onehot = (cb[:, :, None] == np.arange(2 * WIN_C - 1)) & inside[:, :, None]
    toep = jnp.einsum("hrb,cjb->hrcj", rpb.astype(F32), jnp.asarray(onehot, F32),
                      precision=lax.Precision.HIGHEST)
    toep = jnp.where(inside[None, None], toep * math.log2(math.e), NEG_BIAS)
    n_heads = rpb.shape[0]
    variants = []
    for v in range(WIN_R):
        rows = toep[:, WIN_R - 1 - v:2 * WIN_R - 1 - v]
        variants.append(rows.transpose(0, 2, 1, 3).reshape(n_heads // 2, 2 * GRID_W,
                                                           WIN_R * GRID_W))
    return jnp.stack(variants, axis=1)


def _angles(pos, dim, theta):
    inv = jnp.power(theta, -jnp.arange(0, dim, 2, dtype=F32) / dim)
    return pos[:, None] * inv[None, :]


def _rot_tables(parts, seq):
    cos, sin = [], []
    for part in parts:
        if isinstance(part, int):
            cos.append(jnp.ones((seq, part), F32))
            sin.append(jnp.zeros((seq, part), F32))
        else:
            c, s = jnp.cos(part), jnp.sin(part)
            cos += [c, c]
            sin += [-s, s]
    return jnp.concatenate(cos, axis=1), jnp.concatenate(sin, axis=1)


def _dup(v):
    return jnp.concatenate([v, v]).reshape(1, -1).astype(F32)


def _pad_lanes(v):
    return jnp.pad(v.astype(F32), (0, LANES - v.shape[0])).reshape(1, LANES)


def kernel(x, norm_e, w_in_e, gq_a, gk_a, rpb_a, gq_b, gk_b, w_out_e, norm_o, w_in_o, g_cq, w_cq_b, g_ckv, w_ckv_b, gq_c, gk_c, gq_d, gk_d, lam_q1, lam_k1, lam_q2, lam_k2, g_sub_d, w_out_o):
    batch, seq, d = x.shape
    m = batch * seq
    x2 = x.reshape(m, d)
    t = jnp.arange(seq)
    pos = t.astype(F32)
    row = (t // GRID_W).astype(F32)
    col = (t % GRID_W).astype(F32)
    half = D_HEAD // 2
    ones_row = jnp.ones((1, LANES), F32)
    unit_coef = jnp.ones((8, ATTN_TQ), F32)

    n_a = 8 * D_HEAD
    base_b = 3 * n_a
    kb0 = base_b + 8 * D_HEAD
    vb0 = kb0 + 2 * D_HEAD
    gate0 = vb0 + 2 * D_HEAD
    w0 = w_in_e[0]
    dup_heads = [w0[:, c:c + D_HEAD] for c in (kb0, kb0, kb0 + D_HEAD, kb0 + D_HEAD,
                                                 vb0, vb0, vb0 + D_HEAD, vb0 + D_HEAD)]
    w_e = jnp.concatenate([w0[:, :kb0]] + dup_heads + [w0[:, gate0:]],
                          axis=1).astype(BF16)
    ang_row = _angles(row, half, AXIAL_THETA)
    ang_col = _angles(col, half, AXIAL_THETA)
    cos_b, sin_b = _rot_tables([ang_row, ang_col, ang_row, ang_col], seq)
    proj_e, kt_b, vp_b = _proj_in(x2, norm_e[0], w_e, _dup(gk_b[0]), cos_b, sin_b,
                                  batch=batch, seq=seq, k_tiles=(16, 17), v_tiles=(18, 19),
                                  n_groups=2, n_eff=D_HEAD, shift=half // 2)
    bias = _natten_bias(rpb_a[0])
    oa = _natten(proj_e, _dup(gq_a[0]), _dup(gk_a[0]), bias, batch, seq)
    ob = _attention(proj_e, lambda p: (12 + p, 12 + p), kt_b, lambda p: (p // 2, p // 2),
                    vp_b, lambda p: p // 2, _dup(gq_b[0]), cos_b, sin_b, unit_coef, ones_row,
                    batch=batch, seq=seq, n_pairs=4, n_groups=2, n_eff=D_HEAD,
                    shift=half // 2, scale=D_HEAD ** -0.5, split_q=True, combine="select",
                    post_norm=False)
    x2 = _proj_out(oa, ob, proj_e, 5, w_out_e[0].astype(BF16), x2)

    w1 = w_in_o[0]
    q_lora, kv_lora = g_cq.shape[1], g_ckv.shape[1]
    o_kpe = q_lora + kv_lora
    o_qd = o_kpe + C_ROPE
    zeros = functools.partial(jnp.zeros, dtype=w1.dtype)
    w_o = jnp.concatenate([
        w1[:, :o_kpe],
        zeros((d, C_NOPE)), w1[:, o_kpe:o_qd], zeros((d, LANES - C_NOPE - C_ROPE)),
        w1[:, o_qd:]], axis=1).astype(BF16)
    ang_part = _angles(pos, ROT_DIM, ROPE_THETA)
    cos_d, sin_d = _rot_tables([ang_part, D_HEAD - ROT_DIM, ang_part, D_HEAD - ROT_DIM], seq)
    proj_o, kt_d, vp_d = _proj_in(x2, norm_o[0], w_o, _dup(gk_d[0]), cos_d, sin_d,
                                  batch=batch, seq=seq, k_tiles=(8, 9, 10, 11),
                                  v_tiles=(12, 13, 14, 15), n_groups=2, n_eff=D_HEAD,
                                  shift=ROT_DIM // 2)
    n_hc = 8
    dqc = C_NOPE + C_ROPE
    wq = jnp.pad(w_cq_b[0].reshape(q_lora, n_hc, dqc),
                 ((0, 0), (0, 0), (0, LANES - dqc))).reshape(q_lora, n_hc * LANES).astype(BF16)
    wkv = w_ckv_b[0].reshape(kv_lora, n_hc, C_NOPE + C_V)
    wk = jnp.pad(wkv[:, :, :C_NOPE],
                 ((0, 0), (0, 0), (0, LANES - C_NOPE))).reshape(kv_lora, n_hc * LANES).astype(BF16)
    wv = wkv[:, :, C_NOPE:].reshape(kv_lora, n_hc * C_V).astype(BF16)
    ang_mla = _angles(pos, C_ROPE, MLA_THETA)
    cos_c, sin_c = _rot_tables([C_NOPE, ang_mla, LANES - C_NOPE - C_ROPE], seq)
    q_c, kt_c, vp_c = _mla_up(proj_o, g_cq[0], g_ckv[0], wq, wk, wv, _pad_lanes(gk_c[0]),
                              cos_c, sin_c, batch=batch, seq=seq, n_eff=dqc,
                              shift=C_ROPE // 2)
    oc = _attention(q_c, lambda p: (2 * p, 2 * p + 1), kt_c, lambda p: (2 * p, 2 * p + 1),
                    vp_c, lambda p: p, _pad_lanes(gq_c[0]), cos_c, sin_c, unit_coef, ones_row,
                    batch=batch, seq=seq, n_pairs=4, n_groups=1, n_eff=dqc,
                    shift=C_ROPE // 2, scale=dqc ** -0.5, split_q=False, combine="select",
                    post_norm=False)

    lam_init = 0.8 - 0.6 * math.exp(-0.3 * 1)
    lam = (jnp.exp(jnp.sum(lam_q1[0].astype(F32) * lam_k1[0].astype(F32)))
           - jnp.exp(jnp.sum(lam_q2[0].astype(F32) * lam_k2[0].astype(F32))) + lam_init)
    diff_coef = jnp.full((8, ATTN_TQ), -lam, F32)
    gpost = (g_sub_d[0].astype(F32) * (1.0 - lam_init)).reshape(1, LANES)
    od = _attention(proj_o, lambda p: (4 + p, 4 + p), kt_d, lambda p: (p, p),
                    vp_d, lambda p: p, _dup(gq_d[0]), cos_d, sin_d, diff_coef, gpost,
                    batch=batch, seq=seq, n_pairs=4, n_groups=2, n_eff=D_HEAD,
                    shift=ROT_DIM // 2, scale=D_HEAD ** -0.5, split_q=True, combine="add",
                    post_norm=True)
    x2 = _proj_out(oc, od, proj_o, 4, w_out_o[0].astype(BF16), x2)
    return x2.reshape(batch, seq, d)
```

```python
import functools
import math

import jax
import jax.numpy as jnp
import numpy as np
from jax import lax
from jax.experimental import pallas as pl
from jax.experimental.pallas import tpu as pltpu

F32 = jnp.float32
BF16 = jnp.bfloat16

LANES = 128
SUBLANES = 8
VMEM_LIMIT_BYTES = 56 * 1024 * 1024

D_HEAD = 64
GRID_W = 64
RMS_EPS = 1e-6
WIN_R = 8
WIN_C = 16
AXIAL_THETA = 10000.0
MLA_THETA = 10000.0
ROPE_THETA = 500000.0
C_NOPE = 64
C_ROPE = 32
C_V = 64
ROT_DIM = 16
NEG_BIAS = -1e30

PROJ_TM = 512
PROJ_TN = 512
ATTN_TQ = 256
ATTN_TK = 1024
FOLD_WAYS = 4
PREP_ROWS = 512
ONES_ROWS = 16
VT_HALF = D_HEAD + ONES_ROWS


def _params(*sem):
    return pltpu.CompilerParams(dimension_semantics=sem,
                                vmem_limit_bytes=VMEM_LIMIT_BYTES)


def _lane_iota(shape):
    return lax.broadcasted_iota(jnp.int32, shape, len(shape) - 1)


def _group_norm(x, gain, n_groups, n_eff):
    y = x * x
    tot = jnp.sum(y, axis=-1, keepdims=True)
    if n_groups == 1:
        ms = tot * (1.0 / n_eff)
    else:
        lo_mask = _lane_iota(x.shape) < D_HEAD
        lo = jnp.sum(jnp.where(lo_mask, y, 0.0), axis=-1, keepdims=True)
        ms = jnp.where(lo_mask, lo, tot - lo) * (1.0 / D_HEAD)
    return x * lax.rsqrt(ms + RMS_EPS) * gain


def _rotate(x, cos, sin_signed, shift):
    n = x.shape[-1]
    up = pltpu.roll(x, n - shift, 1)
    down = pltpu.roll(x, shift, 1)
    partner = jnp.where((_lane_iota(x.shape) & shift) == 0, up, down)
    return x * cos + partner * sin_signed


def _store_keys(kn_ref, j, tile, gain, cos, sin, *, n_groups, n_eff, shift, rotate=True):
    k = _group_norm(tile, gain, n_groups, n_eff)
    if rotate:
        k = _rotate(k, cos, sin, shift)
    kn_ref[0, j] = k.astype(BF16)


def _store_values_t(vt_ref, j, tile):
    vt = tile.T.astype(BF16)
    ones = jnp.ones((ONES_ROWS, vt.shape[1]), BF16)
    for h in range(2):
        base = h * VT_HALF
        vt_ref[0, j, base:base + D_HEAD, :] = vt[h * D_HEAD:(h + 1) * D_HEAD]
        vt_ref[0, j, base + D_HEAD:base + VT_HALF, :] = ones


def _proj_in_kernel(x_ref, g_ref, w_ref, gk_ref, cos_ref, sin_ref, o_ref, kn_ref, vt_ref, *,
                    k_tiles, v_tiles, n_groups, n_eff, shift):
    x = x_ref[...]
    ms = jnp.mean(x * x, axis=-1, keepdims=True)
    h = (x * lax.rsqrt(ms + RMS_EPS) * g_ref[...]).astype(BF16)
    n = o_ref.shape[1]
    for n0 in range(0, n, PROJ_TN):
        o_ref[:, n0:n0 + PROJ_TN] = jnp.dot(
            h, w_ref[:, n0:n0 + PROJ_TN], preferred_element_type=F32)
        for j, (t, g, rotate) in enumerate(k_tiles):
            if n0 <= t * LANES < n0 + PROJ_TN:
                _store_keys(kn_ref, j, o_ref[:, t * LANES:(t + 1) * LANES],
                            gk_ref[g * SUBLANES:g * SUBLANES + 1, :],
                            cos_ref[...], sin_ref[...], n_groups=n_groups, n_eff=n_eff,
                            shift=shift, rotate=rotate)
        for j, t in enumerate(v_tiles):
            if n0 <= t * LANES < n0 + PROJ_TN:
                _store_values_t(vt_ref, j, o_ref[:, t * LANES:(t + 1) * LANES])


def _proj_in(x2, gain, w, gk, cos, sin, *, batch, seq, k_tiles, v_tiles, n_groups, n_eff,
             shift):
    m, d = x2.shape
    n = w.shape[1]
    tm = PROJ_TM
    nblk = seq // tm
    nk, nv = len(k_tiles), len(v_tiles)
    kern = functools.partial(_proj_in_kernel, k_tiles=k_tiles, v_tiles=v_tiles,
                             n_groups=n_groups, n_eff=n_eff, shift=shift)
    return pl.pallas_call(
        kern,
        grid=(m // tm,),
        in_specs=[pl.BlockSpec((tm, d), lambda i: (i, 0)),
                  pl.BlockSpec((1, d), lambda i: (0, 0)),
                  pl.BlockSpec((d, n), lambda i: (0, 0)),
                  pl.BlockSpec((gk.shape[0] * SUBLANES, LANES), lambda i: (0, 0)),
                  pl.BlockSpec((tm, LANES), lambda i: (i % nblk, 0)),
                  pl.BlockSpec((tm, LANES), lambda i: (i % nblk, 0))],
        out_specs=[pl.BlockSpec((tm, n), lambda i: (i, 0)),
                   pl.BlockSpec((1, nk, tm, LANES), lambda i: (i // nblk, 0, i % nblk, 0)),
                   pl.BlockSpec((1, nv, 2 * VT_HALF, tm),
                                lambda i: (i // nblk, 0, 0, i % nblk))],
        out_shape=[jax.ShapeDtypeStruct((m, n), F32),
                   jax.ShapeDtypeStruct((batch, nk, seq, LANES), BF16),
                   jax.ShapeDtypeStruct((batch, nv, 2 * VT_HALF, seq), BF16)],
        compiler_params=_params("parallel"),
        name="proj_in",
    )(x2, gain.reshape(1, d), w, jnp.repeat(gk, SUBLANES, axis=0), cos, sin)


def _proj_out_kernel(oa_ref, ob_ref, ga_ref, gb_ref, w_ref, x_ref, o_ref):
    def gated(o_r, g_r):
        g = g_r[...]
        return (o_r[...] * (g / (1.0 + jnp.exp(-g)))).astype(BF16)

    half = oa_ref.shape[1]
    y = jnp.dot(gated(oa_ref, ga_ref), w_ref[:half, :], preferred_element_type=F32)
    y = y + jnp.dot(gated(ob_ref, gb_ref), w_ref[half:, :], preferred_element_type=F32)
    o_ref[...] = x_ref[...] + y


def _proj_out(oa, ob, proj, gate_blk, w, x2):
    m, d = x2.shape
    half = oa.shape[1]
    return pl.pallas_call(
        _proj_out_kernel,
        grid=(m // PROJ_TM,),
        in_specs=[pl.BlockSpec((PROJ_TM, half), lambda i: (i, 0)),
                  pl.BlockSpec((PROJ_TM, half), lambda i: (i, 0)),
                  pl.BlockSpec((PROJ_TM, half), lambda i: (i, gate_blk)),
                  pl.BlockSpec((PROJ_TM, half), lambda i: (i, gate_blk + 1)),
                  pl.BlockSpec((2 * half, d), lambda i: (0, 0)),
                  pl.BlockSpec((PROJ_TM, d), lambda i: (i, 0))],
        out_specs=pl.BlockSpec((PROJ_TM, d), lambda i: (i, 0)),
        out_shape=jax.ShapeDtypeStruct((m, d), F32),
        compiler_params=_params("parallel"),
        name="proj_out",
    )(oa, ob, proj, proj, w, x2)


def _mla_up_kernel(cq_ref, ckv_ref, kpe_ref, gq_ref, gkv_ref, wq_ref, wk_ref, wv_ref,
                   gk_ref, cos_ref, sin_ref, q_ref, kn_ref, vt_ref, *, n_eff, shift):
    def normed(x, g):
        ms = jnp.mean(x * x, axis=-1, keepdims=True)
        return (x * lax.rsqrt(ms + RMS_EPS) * g).astype(BF16)

    hq = normed(cq_ref[...], gq_ref[...])
    hkv = normed(ckv_ref[...], gkv_ref[...])
    q_ref[...] = jnp.dot(hq, wq_ref[...], preferred_element_type=F32)
    kk = jnp.dot(hkv, wk_ref[...], preferred_element_type=F32)
    kpe = kpe_ref[...]
    gk = gk_ref[...]
    pe_rot = _rotate(kpe * gk, cos_ref[...], sin_ref[...], shift)
    ss_pe = jnp.sum(kpe * kpe, axis=-1, keepdims=True)
    for h in range(kn_ref.shape[1]):
        nope = kk[:, h * LANES:(h + 1) * LANES]
        ss = jnp.sum(nope * nope, axis=-1, keepdims=True) + ss_pe
        r = lax.rsqrt(ss * (1.0 / n_eff) + RMS_EPS)
        kn_ref[0, h] = ((nope * gk + pe_rot) * r).astype(BF16)
    vv = jnp.dot(hkv, wv_ref[...], preferred_element_type=F32)
    for j in range(vt_ref.shape[1]):
        _store_values_t(vt_ref, j, vv[:, j * LANES:(j + 1) * LANES])


def _mla_up(proj, g_cq, g_ckv, wq, wk, wv, gk, cos, sin, *, batch, seq, n_eff, shift):
    m = proj.shape[0]
    ql, kvl = wq.shape[0], wk.shape[0]
    tm = PROJ_TM
    nblk = seq // tm
    nk, nv = wk.shape[1] // LANES, wv.shape[1] // LANES
    return pl.pallas_call(
        functools.partial(_mla_up_kernel, n_eff=n_eff, shift=shift),
        grid=(m // tm,),
        in_specs=[pl.BlockSpec((tm, ql), lambda i: (i, 0)),
                  pl.BlockSpec((tm, kvl), lambda i: (i, ql // kvl)),
                  pl.BlockSpec((tm, LANES), lambda i: (i, (ql + kvl) // LANES)),
                  pl.BlockSpec((1, ql), lambda i: (0, 0)),
                  pl.BlockSpec((1, kvl), lambda i: (0, 0)),
                  pl.BlockSpec(wq.shape, lambda i: (0, 0)),
                  pl.BlockSpec(wk.shape, lambda i: (0, 0)),
                  pl.BlockSpec(wv.shape, lambda i: (0, 0)),
                  pl.BlockSpec((1, LANES), lambda i: (0, 0)),
                  pl.BlockSpec((tm, LANES), lambda i: (i % nblk, 0)),
                  pl.BlockSpec((tm, LANES), lambda i: (i % nblk, 0))],
        out_specs=[pl.BlockSpec((tm, wq.shape[1]), lambda i: (i, 0)),
                   pl.BlockSpec((1, nk, tm, LANES), lambda i: (i // nblk, 0, i % nblk, 0)),
                   pl.BlockSpec((1, nv, 2 * VT_HALF, tm),
                                lambda i: (i // nblk, 0, 0, i % nblk))],
        out_shape=[jax.ShapeDtypeStruct((m, wq.shape[1]), F32),
                   jax.ShapeDtypeStruct((batch, nk, seq, LANES), BF16),
                   jax.ShapeDtypeStruct((batch, nv, 2 * VT_HALF, seq), BF16)],
        compiler_params=_params("parallel"),
        name="mla_up",
    )(proj, proj, proj, g_cq.reshape(1, ql), g_ckv.reshape(1, kvl), wq, wk, wv, gk, cos, sin)


def _attn_kernel(qe_ref, qo_ref, gq_ref, cos_ref, sin_ref, ke_ref, ko_ref, vt_ref,
                 coef_ref, gpost_ref, o_ref, s0_scr, s1_scr, qt_scr, res_scr,
                 *, n_groups, n_eff, shift, qscale, split_q, combine, post_norm):
    seq = vt_ref.shape[3]
    tq = ATTN_TQ
    nq = seq // tq
    rv = res_scr.shape[1]
    q_refs, k_refs = (qe_ref, qo_ref), (ke_ref, ko_ref)
    s_scrs = (s0_scr, s1_scr)
    lane = _lane_iota((tq, LANES))

    def block_rows(i):
        if isinstance(i, int):
            return pl.ds(i * tq, tq)
        return pl.ds(pl.multiple_of(i * tq, tq), tq)

    def fold8(x, op):
        part = op(x.reshape(FOLD_WAYS, x.shape[0] // (8 * FOLD_WAYS), 8, x.shape[1]), axis=1)
        return op(part, axis=0)

    def tree(parts, op):
        while len(parts) > 1:
            parts = [op(a, b) for a, b in zip(parts[::2], parts[1::2])] + parts[len(parts) & ~1:]
        return parts[0]

    def prep_q(i, u):
        rows = block_rows(i)
        q = _group_norm(q_refs[u][rows, :], gq_ref[...], n_groups, n_eff)
        q = _rotate(q, cos_ref[rows, :], sin_ref[rows, :], shift) * qscale
        if split_q:
            q = jnp.where((lane < D_HEAD) == (u == 0), q, 0.0)
        qt_scr[u] = q.T.astype(BF16)

    def tick(scores=None, softmax=None):
        acc = None
        maxes = []
        if scores is not None:
            qt = qt_scr[scores]
        for c0 in range(0, seq, ATTN_TK):
            ks = slice(c0, c0 + ATTN_TK)
            if scores is not None:
                s = jnp.dot(k_refs[scores][0, 0, ks, :], qt, preferred_element_type=F32)
                s_scrs[scores][ks, :] = s
                maxes.append(fold8(s, jnp.max))
            if softmax is not None:
                u, m = softmax
                p = jnp.exp2(s_scrs[u][ks, :] - m).astype(BF16)
                vrows = (slice(u * VT_HALF, (u + 1) * VT_HALF)
                         if combine == "select" else slice(0, VT_HALF + D_HEAD))
                pv = jnp.dot(vt_ref[0, 0, vrows, ks], p, preferred_element_type=F32)
                acc = pv if acc is None else acc + pv
        m_new = jnp.max(tree(maxes, jnp.maximum), axis=0, keepdims=True) if maxes else None
        res = None
        if acc is not None:
            inv_l = 1.0 / acc[D_HEAD:D_HEAD + 1]
            halves = [acc[h:h + D_HEAD] for h in range(0, acc.shape[0], VT_HALF)]
            res = jnp.concatenate(halves, axis=0) * inv_l
        return m_new, res

    def emit(i, res_e, res_o):
        if combine == "select":
            out_t = jnp.concatenate([res_e, res_o], axis=0)
        else:
            out_t = res_e + coef_ref[0:1, :] * res_o
        if post_norm:
            ms = jnp.mean(out_t * out_t, axis=0, keepdims=True)
            out_t = out_t * lax.rsqrt(ms + RMS_EPS)
        o_ref[block_rows(i), :] = out_t.T * gpost_ref[...]

    prep_q(0, 0)
    prep_q(0, 1)
    m_e, _ = tick(scores=0)
    prep_q(min(1, nq - 1), 0)
    m_o, res_e = tick(scores=1, softmax=(0, m_e))
    res_scr[0] = res_e
    prep_q(min(1, nq - 1), 1)

    def body(i, m_o):
        m_e_next, res_o = tick(scores=0, softmax=(1, m_o))
        ahead = jnp.minimum(i + 2, nq - 1)
        prep_q(ahead, 0)
        emit(i, res_scr[0], res_o)
        m_o_next, res_e = tick(scores=1, softmax=(0, m_e_next))
        prep_q(ahead, 1)
        res_scr[0] = res_e
        return m_o_next

    m_o = lax.fori_loop(0, nq - 1, body, m_o)
    _, res_o = tick(softmax=(1, m_o))
    emit(nq - 1, res_scr[0], res_o)


def _attention(q_src, q_blk, kn, k_idx, vt, v_idx, gq, cos, sin, coef, gpost, *,
               batch, seq, n_pairs, n_groups, n_eff, shift, scale, split_q, combine,
               post_norm):
    tq = ATTN_TQ
    m = batch * seq

    def qmap(which):
        return lambda b, p: (b, q_blk(p)[which])

    def kmap(which):
        return lambda b, p: (b, k_idx(p)[which], 0, 0)

    kern = functools.partial(_attn_kernel, n_groups=n_groups, n_eff=n_eff, shift=shift,
                             qscale=scale * math.log2(math.e), split_q=split_q,
                             combine=combine, post_norm=post_norm)
    return pl.pallas_call(
        kern,
        grid=(batch, n_pairs),
        in_specs=[pl.BlockSpec((seq, LANES), qmap(0)),
                  pl.BlockSpec((seq, LANES), qmap(1)),
                  pl.BlockSpec((1, LANES), lambda b, p: (0, 0)),
                  pl.BlockSpec((seq, LANES), lambda b, p: (0, 0),
                               pipeline_mode=pl.Buffered(1)),
                  pl.BlockSpec((seq, LANES), lambda b, p: (0, 0),
                               pipeline_mode=pl.Buffered(1)),
                  pl.BlockSpec((1, 1, seq, LANES), kmap(0)),
                  pl.BlockSpec((1, 1, seq, LANES), kmap(1)),
                  pl.BlockSpec((1, 1, 2 * VT_HALF, seq), lambda b, p: (b, v_idx(p), 0, 0)),
                  pl.BlockSpec((8, tq), lambda b, p: (0, 0)),
                  pl.BlockSpec((1, LANES), lambda b, p: (0, 0))],
        out_specs=pl.BlockSpec((seq, LANES), lambda b, p: (b, p)),
        out_shape=jax.ShapeDtypeStruct((m, n_pairs * LANES), F32),
        scratch_shapes=[pltpu.VMEM((seq, tq), F32), pltpu.VMEM((seq, tq), F32),
                        pltpu.VMEM((2, LANES, tq), BF16),
                        pltpu.VMEM((1, D_HEAD if combine == "select" else LANES, tq), F32)],
        compiler_params=_params("parallel", "parallel"),
        name="attention",
    )(q_src, q_src, gq, cos, sin, kn, kn, vt, coef, gpost)


def _natten_kernel(q_ref, k_ref, v_ref, bias_ref, o_ref, q_scr, v_scr, *, rows):
    seq = v_ref.shape[0]
    nwin = WIN_R * GRID_W
    for r0 in range(0, seq, PREP_ROWS):
        sl = slice(r0, r0 + PREP_ROWS)
        q = q_ref[0, 0, sl, :]
        lo = _lane_iota(q.shape) < D_HEAD
        zero = jnp.zeros_like(q)
        q_scr[0, sl, :] = jnp.where(lo, q, zero)
        q_scr[1, sl, :] = jnp.where(lo, zero, q)
        v_scr[sl, :LANES] = v_ref[sl, :].astype(BF16)
        v_scr[sl, LANES:] = jnp.ones((PREP_ROWS, LANES), BF16)

    lane_lo = _lane_iota((GRID_W, LANES)) < D_HEAD

    def body(r, carry):
        w0 = jnp.clip(r - WIN_R // 2, 0, rows - WIN_R)
        var = r - w0
        kbase = pl.multiple_of(w0 * GRID_W, GRID_W)
        qbase = pl.multiple_of(r * GRID_W, GRID_W)
        kw = k_ref[0, 0, pl.ds(kbase, nwin), :]
        vw = v_scr[pl.ds(kbase, nwin), :]
        qrows = pl.ds(qbase, GRID_W)
        q2 = jnp.concatenate([q_scr[0, qrows, :], q_scr[1, qrows, :]], axis=0)
        s = lax.dot_general(q2, kw, (((1,), (1,)), ((), ())),
                            preferred_element_type=F32)
        z = s + bias_ref[0, var]
        zmax = jnp.max(z, axis=-1, keepdims=True)
        p = jnp.exp2(z - zmax).astype(BF16)
        acc = jnp.dot(p, vw, preferred_element_type=F32)
        res = acc[:, :LANES] / acc[:, LANES:]
        o_ref[qrows, :] = jnp.where(lane_lo, res[:GRID_W], res[GRID_W:])
        return carry

    lax.fori_loop(0, rows, body, 0, unroll=16)


def _natten(tiles, proj, v_blk0, bias, batch, seq):
    rows = seq // GRID_W
    n_pairs = bias.shape[0]
    nwin = WIN_R * GRID_W
    return pl.pallas_call(
        functools.partial(_natten_kernel, rows=rows),
        grid=(batch, n_pairs),
        in_specs=[pl.BlockSpec((1, 1, seq, LANES), lambda b, p: (b, p, 0, 0)),
                  pl.BlockSpec((1, 1, seq, LANES), lambda b, p: (b, n_pairs + p, 0, 0)),
                  pl.BlockSpec((seq, LANES), lambda b, p: (b, v_blk0 + p)),
                  pl.BlockSpec((1, WIN_R, 2 * GRID_W, nwin), lambda b, p: (p, 0, 0, 0))],
        out_specs=pl.BlockSpec((seq, LANES), lambda b, p: (b, p)),
        out_shape=jax.ShapeDtypeStruct((batch * seq, n_pairs * LANES), F32),
        scratch_shapes=[pltpu.VMEM((2, seq, LANES), BF16),
                        pltpu.VMEM((seq, 2 * LANES), BF16)],
        compiler_params=_params("parallel", "parallel"),
        name="natten",
    )(tiles, tiles, proj, bias)


def _natten_bias(rpb):
    col = np.arange(GRID_W)
    c0 = np.clip(col - WIN_C // 2, 0, GRID_W - WIN_C)
    ci = np.arange(GRID_W)
    inside = (ci[None, :] >= c0[:, None]) & (ci[None, :] < c0[:, None] + WIN_C)
    cb = ci[None, :] - col[:, None] + (WIN_C - 1)
    onehot = (cb[:, :, None] == np.arange(2 * WIN_C - 1)) & inside[:, :, None]
    toep = jnp.einsum("hrb,cjb->hrcj", rpb.astype(F32), jnp.asarray(onehot, F32),
                      precision=lax.Precision.HIGHEST)
    toep = jnp.where(inside[None, None], toep * math.log2(math.e), NEG_BIAS)
    n_heads = rpb.shape[0]
    variants = []
    for v in range(WIN_R):
        rows = toep[:, WIN_R - 1 - v:2 * WIN_R - 1 - v]
        variants.append(rows.transpose(0, 2, 1, 3).reshape(n_heads // 2, 2 * GRID_W,
                                                           WIN_R * GRID_W))
    return jnp.stack(variants, axis=1)


def _rot_tables(parts):
    ang = 0.0
    sign, off = [], 0
    for part in parts:
        if isinstance(part, int):
            sign.append(np.zeros(part, np.float32))
            off += part
            continue
        pos, dim, theta = part
        inv = jnp.power(theta, -jnp.arange(0, dim, 2, dtype=F32) / dim)
        inv_lanes = jnp.pad(jnp.concatenate([inv, inv]), (off, LANES - off - dim))
        ang = ang + pos[:, None] * inv_lanes[None, :]
        sign.append(np.repeat(np.float32([-1.0, 1.0]), dim // 2))
        off += dim
    return jnp.cos(ang), jnp.sin(ang) * jnp.asarray(np.concatenate(sign))[None, :]


def _dup(v):
    return jnp.concatenate([v, v]).reshape(1, -1).astype(F32)


def _pad_lanes(v):
    return jnp.pad(v.astype(F32), (0, LANES - v.shape[0])).reshape(1, LANES)


def kernel(x, norm_e, w_in_e, gq_a, gk_a, rpb_a, gq_b, gk_b, w_out_e, norm_o, w_in_o, g_cq, w_cq_b, g_ckv, w_ckv_b, gq_c, gk_c, gq_d, gk_d, lam_q1, lam_k1, lam_q2, lam_k2, g_sub_d, w_out_o):
    batch, seq, d = x.shape
    m = batch * seq
    x2 = x.reshape(m, d)
    t = jnp.arange(seq)
    pos = t.astype(F32)
    row = (t // GRID_W).astype(F32)
    col = (t % GRID_W).astype(F32)
    half = D_HEAD // 2
    ones_row = jnp.ones((1, LANES), F32)
    unit_coef = jnp.ones((8, ATTN_TQ), F32)

    n_a = 8 * D_HEAD
    base_b = 3 * n_a
    kb0 = base_b + 8 * D_HEAD
    vb0 = kb0 + 2 * D_HEAD
    gate0 = vb0 + 2 * D_HEAD
    w0 = w_in_e[0]
    dup_heads = [w0[:, c:c + D_HEAD] for c in (kb0, kb0, kb0 + D_HEAD, kb0 + D_HEAD,
                                                 vb0, vb0, vb0 + D_HEAD, vb0 + D_HEAD)]
    w_e = jnp.concatenate([w0[:, :kb0]] + dup_heads + [w0[:, gate0:]],
                          axis=1).astype(BF16)
    axial = [(row, half, AXIAL_THETA), (col, half, AXIAL_THETA)]
    cos_b, sin_b = _rot_tables(axial + axial)
    gains_e = jnp.concatenate([_dup(gq_a[0]) * (D_HEAD ** -0.5 * math.log2(math.e)),
                               _dup(gk_a[0]), _dup(gk_b[0])], axis=0)
    tiles_e = (tuple((t, 0, False) for t in range(0, 4))
               + tuple((t, 1, False) for t in range(4, 8)) + ((16, 2, True), (17, 2, True)))
    proj_e, kt_e, vp_b = _proj_in(x2, norm_e[0], w_e, gains_e, cos_b, sin_b,
                                  batch=batch, seq=seq, k_tiles=tiles_e, v_tiles=(18, 19),
                                  n_groups=2, n_eff=D_HEAD, shift=half // 2)
    bias = _natten_bias(rpb_a[0])
    oa = _natten(kt_e, proj_e, 8, bias, batch, seq)
    ob = _attention(proj_e, lambda p: (12 + p, 12 + p), kt_e,
                    lambda p: (8 + p // 2, 8 + p // 2),
                    vp_b, lambda p: p // 2, _dup(gq_b[0]), cos_b, sin_b, unit_coef, ones_row,
                    batch=batch, seq=seq, n_pairs=4, n_groups=2, n_eff=D_HEAD,
                    shift=half // 2, scale=D_HEAD ** -0.5, split_q=True, combine="select",
                    post_norm=False)
    x2 = _proj_out(oa, ob, proj_e, 5, w_out_e[0].astype(BF16), x2)

    w1 = w_in_o[0]
    q_lora, kv_lora = g_cq.shape[1], g_ckv.shape[1]
    o_kpe = q_lora + kv_lora
    o_qd = o_kpe + C_ROPE
    zeros = functools.partial(jnp.zeros, dtype=w1.dtype)
    w_o = jnp.concatenate([
        w1[:, :o_kpe],
        zeros((d, C_NOPE)), w1[:, o_kpe:o_qd], zeros((d, LANES - C_NOPE - C_ROPE)),
        w1[:, o_qd:]], axis=1).astype(BF16)
    partial = [(pos, ROT_DIM, ROPE_THETA), D_HEAD - ROT_DIM]
    cos_d, sin_d = _rot_tables(partial + partial)
    proj_o, kt_d, vp_d = _proj_in(x2, norm_o[0], w_o, _dup(gk_d[0]), cos_d, sin_d,
                                  batch=batch, seq=seq,
                                  k_tiles=tuple((t, 0, True) for t in range(8, 12)),
                                  v_tiles=(12, 13, 14, 15), n_groups=2, n_eff=D_HEAD,
                                  shift=ROT_DIM // 2)
    n_hc = 8
    dqc = C_NOPE + C_ROPE
    wq = jnp.pad(w_cq_b[0].reshape(q_lora, n_hc, dqc),
                 ((0, 0), (0, 0), (0, LANES - dqc))).reshape(q_lora, n_hc * LANES).astype(BF16)
    wkv = w_ckv_b[0].reshape(kv_lora, n_hc, C_NOPE + C_V)
    wk = jnp.pad(wkv[:, :, :C_NOPE],
                 ((0, 0), (0, 0), (0, LANES - C_NOPE))).reshape(kv_lora, n_hc * LANES).astype(BF16)
    wv = wkv[:, :, C_NOPE:].reshape(kv_lora, n_hc * C_V).astype(BF16)
    cos_c, sin_c = _rot_tables([C_NOPE, (pos, C_ROPE, MLA_THETA), LANES - C_NOPE - C_ROPE])
    q_c, kt_c, vp_c = _mla_up(proj_o, g_cq[0], g_ckv[0], wq, wk, wv, _pad_lanes(gk_c[0]),
                              cos_c, sin_c, batch=batch, seq=seq, n_eff=dqc,
                              shift=C_ROPE // 2)
    oc = _attention(q_c, lambda p: (2 * p, 2 * p + 1), kt_c, lambda p: (2 * p, 2 * p + 1),
                    vp_c, lambda p: p, _pad_lanes(gq_c[0]), cos_c, sin_c, unit_coef, ones_row,
                    batch=batch, seq=seq, n_pairs=4, n_groups=1, n_eff=dqc,
                    shift=C_ROPE // 2, scale=dqc ** -0.5, split_q=False, combine="select",
                    post_norm=False)

    lam_init = 0.8 - 0.6 * math.exp(-0.3 * 1)
    lam = (jnp.exp(jnp.sum(lam_q1[0].astype(F32) * lam_k1[0].astype(F32)))
           - jnp.exp(jnp.sum(lam_q2[0].astype(F32) * lam_k2[0].astype(F32))) + lam_init)
    diff_coef = jnp.full((8, ATTN_TQ), -lam, F32)
    gpost = (g_sub_d[0].astype(F32) * (1.0 - lam_init)).reshape(1, LANES)
    od = _attention(proj_o, lambda p: (4 + p, 4 + p), kt_d, lambda p: (p, p),
                    vp_d, lambda p: p, _dup(gq_d[0]), cos_d, sin_d, diff_coef, gpost,
                    batch=batch, seq=seq, n_pairs=4, n_groups=2, n_eff=D_HEAD,
                    shift=ROT_DIM // 2, scale=D_HEAD ** -0.5, split_q=True, combine="add",
                    post_norm=True)
    x2 = _proj_out(oc, od, proj_o, 4, w_out_o[0].astype(BF16), x2)
    return x2.reshape(batch, seq, d)
```

```python
import functools
import math

import jax
import jax.numpy as jnp
import numpy as np
from jax import lax
from jax.experimental import pallas as pl
from jax.experimental.pallas import tpu as pltpu

F32 = jnp.float32
BF16 = jnp.bfloat16

LANES = 128
SUBLANES = 8
VMEM_LIMIT_BYTES = 56 * 1024 * 1024

D_HEAD = 64
GRID_W = 64
RMS_EPS = 1e-6
WIN_R = 8
WIN_C = 16
AXIAL_THETA = 10000.0
MLA_THETA = 10000.0
ROPE_THETA = 500000.0
C_NOPE = 64
C_ROPE = 32
C_V = 64
ROT_DIM = 16
NEG_BIAS = -1e30

PROJ_TM = 512
PROJ_TN = 512
ATTN_TQ = 256
ATTN_TK = 1024
ATTN_TKQ = {"select": 512, "add": 1024}
FOLD_WAYS = 4
PREP_ROWS = 512
ONES_ROWS = 16
VT_HALF = D_HEAD + ONES_ROWS


def _params(*sem):
    return pltpu.CompilerParams(dimension_semantics=sem,
                                vmem_limit_bytes=VMEM_LIMIT_BYTES)


def _lane_iota(shape):
    return lax.broadcasted_iota(jnp.int32, shape, len(shape) - 1)


def _group_norm(x, gain, n_groups, n_eff):
    y = x * x
    tot = jnp.sum(y, axis=-1, keepdims=True)
    if n_groups == 1:
        ms = tot * (1.0 / n_eff)
    else:
        lo_mask = _lane_iota(x.shape) < D_HEAD
        lo = jnp.sum(jnp.where(lo_mask, y, 0.0), axis=-1, keepdims=True)
        ms = jnp.where(lo_mask, lo, tot - lo) * (1.0 / D_HEAD)
    return x * lax.rsqrt(ms + RMS_EPS) * gain


def _rotate(x, cos, sin_signed, shift):
    n = x.shape[-1]
    up = pltpu.roll(x, n - shift, 1)
    down = pltpu.roll(x, shift, 1)
    partner = jnp.where((_lane_iota(x.shape) & shift) == 0, up, down)
    return x * cos + partner * sin_signed


def _store_keys(kn_ref, j, tile, gain, cos, sin, *, n_groups, n_eff, shift, rotate=True):
    k = _group_norm(tile, gain, n_groups, n_eff)
    if rotate:
        k = _rotate(k, cos, sin, shift)
    kn_ref[0, j] = k.astype(BF16)


def _store_values_t(vt_ref, j, tile):
    vt = tile.T.astype(BF16)
    ones = jnp.ones((ONES_ROWS, vt.shape[1]), BF16)
    for h in range(2):
        base = h * VT_HALF
        vt_ref[0, j, base:base + D_HEAD, :] = vt[h * D_HEAD:(h + 1) * D_HEAD]
        vt_ref[0, j, base + D_HEAD:base + VT_HALF, :] = ones


def _proj_in_kernel(x_ref, g_ref, w_ref, gk_ref, cos_ref, sin_ref, o_ref, gate_ref, kn_ref,
                    vt_ref, *, k_tiles, v_tiles, n_groups, n_eff, shift):
    x = x_ref[...]
    ms = jnp.mean(x * x, axis=-1, keepdims=True)
    h = (x * lax.rsqrt(ms + RMS_EPS) * g_ref[...]).astype(BF16)
    n_mix = o_ref.shape[1]
    for n0 in range(0, w_ref.shape[1], PROJ_TN):
        y = jnp.dot(h, w_ref[:, n0:n0 + PROJ_TN], preferred_element_type=F32)
        if n0 >= n_mix:
            gate_ref[:, n0 - n_mix:n0 - n_mix + PROJ_TN] = y.astype(BF16)
            continue
        o_ref[:, n0:n0 + PROJ_TN] = y
        for j, (t, g, rotate) in enumerate(k_tiles):
            if n0 <= t * LANES < n0 + PROJ_TN:
                _store_keys(kn_ref, j, o_ref[:, t * LANES:(t + 1) * LANES],
                            gk_ref[g * SUBLANES:g * SUBLANES + 1, :],
                            cos_ref[...], sin_ref[...], n_groups=n_groups, n_eff=n_eff,
                            shift=shift, rotate=rotate)
        for j, t in enumerate(v_tiles):
            if n0 <= t * LANES < n0 + PROJ_TN:
                _store_values_t(vt_ref, j, o_ref[:, t * LANES:(t + 1) * LANES])


def _proj_in(x2, gain, w, gk, cos, sin, *, batch, seq, n_gate, k_tiles, v_tiles, n_groups,
             n_eff, shift):
    m, d = x2.shape
    n = w.shape[1]
    n_mix = n - n_gate
    tm = PROJ_TM
    nblk = seq // tm
    nk, nv = len(k_tiles), len(v_tiles)
    kern = functools.partial(_proj_in_kernel, k_tiles=k_tiles, v_tiles=v_tiles,
                             n_groups=n_groups, n_eff=n_eff, shift=shift)
    return pl.pallas_call(
        kern,
        grid=(m // tm,),
        in_specs=[pl.BlockSpec((tm, d), lambda i: (i, 0)),
                  pl.BlockSpec((1, d), lambda i: (0, 0)),
                  pl.BlockSpec((d, n), lambda i: (0, 0)),
                  pl.BlockSpec((gk.shape[0] * SUBLANES, LANES), lambda i: (0, 0)),
                  pl.BlockSpec((tm, LANES), lambda i: (i % nblk, 0)),
                  pl.BlockSpec((tm, LANES), lambda i: (i % nblk, 0))],
        out_specs=[pl.BlockSpec((tm, n_mix), lambda i: (i, 0)),
                   pl.BlockSpec((tm, n_gate), lambda i: (i, 0)),
                   pl.BlockSpec((1, nk, tm, LANES), lambda i: (i // nblk, 0, i % nblk, 0)),
                   pl.BlockSpec((1, nv, 2 * VT_HALF, tm),
                                lambda i: (i // nblk, 0, 0, i % nblk))],
        out_shape=[jax.ShapeDtypeStruct((m, n_mix), F32),
                   jax.ShapeDtypeStruct((m, n_gate), BF16),
                   jax.ShapeDtypeStruct((batch, nk, seq, LANES), BF16),
                   jax.ShapeDtypeStruct((batch, nv, 2 * VT_HALF, seq), BF16)],
        compiler_params=_params("parallel"),
        name="proj_in",
    )(x2, gain.reshape(1, d), w, jnp.repeat(gk, SUBLANES, axis=0), cos, sin)


def _proj_out_kernel(oa_ref, ob_ref, ga_ref, gb_ref, w_ref, x_ref, o_ref):
    def gated(o_r, g_r):
        g = g_r[...].astype(F32)
        return (o_r[...].astype(F32) * (g / (1.0 + jnp.exp(-g)))).astype(BF16)

    half = oa_ref.shape[1]
    y = jnp.dot(gated(oa_ref, ga_ref), w_ref[:half, :], preferred_element_type=F32)
    y = y + jnp.dot(gated(ob_ref, gb_ref), w_ref[half:, :], preferred_element_type=F32)
    o_ref[...] = x_ref[...] + y


def _proj_out(oa, ob, gate, w, x2):
    m, d = x2.shape
    half = oa.shape[1]
    return pl.pallas_call(
        _proj_out_kernel,
        grid=(m // PROJ_TM,),
        in_specs=[pl.BlockSpec((PROJ_TM, half), lambda i: (i, 0)),
                  pl.BlockSpec((PROJ_TM, half), lambda i: (i, 0)),
                  pl.BlockSpec((PROJ_TM, half), lambda i: (i, 0)),
                  pl.BlockSpec((PROJ_TM, half), lambda i: (i, 1)),
                  pl.BlockSpec((2 * half, d), lambda i: (0, 0)),
                  pl.BlockSpec((PROJ_TM, d), lambda i: (i, 0))],
        out_specs=pl.BlockSpec((PROJ_TM, d), lambda i: (i, 0)),
        out_shape=jax.ShapeDtypeStruct((m, d), F32),
        compiler_params=_params("parallel"),
        name="proj_out",
    )(oa, ob, gate, gate, w, x2)


def _mla_up_kernel(cq_ref, ckv_ref, kpe_ref, gq_ref, gkv_ref, wq_ref, wk_ref, wv_ref,
                   gk_ref, cos_ref, sin_ref, q_ref, kn_ref, vt_ref, *, n_eff, shift):
    def normed(x, g):
        ms = jnp.mean(x * x, axis=-1, keepdims=True)
        return (x * lax.rsqrt(ms + RMS_EPS) * g).astype(BF16)

    hq = normed(cq_ref[...], gq_ref[...])
    hkv = normed(ckv_ref[...], gkv_ref[...])
    q_ref[...] = jnp.dot(hq, wq_ref[...], preferred_element_type=F32)
    kk = jnp.dot(hkv, wk_ref[...], preferred_element_type=F32)
    kpe = kpe_ref[...]
    gk = gk_ref[...]
    pe_rot = _rotate(kpe * gk, cos_ref[...], sin_ref[...], shift)
    ss_pe = jnp.sum(kpe * kpe, axis=-1, keepdims=True)
    for h in range(kn_ref.shape[1]):
        nope = kk[:, h * LANES:(h + 1) * LANES]
        ss = jnp.sum(nope * nope, axis=-1, keepdims=True) + ss_pe
        r = lax.rsqrt(ss * (1.0 / n_eff) + RMS_EPS)
        kn_ref[0, h] = ((nope * gk + pe_rot) * r).astype(BF16)
    vv = jnp.dot(hkv, wv_ref[...], preferred_element_type=F32)
    for j in range(vt_ref.shape[1]):
        _store_values_t(vt_ref, j, vv[:, j * LANES:(j + 1) * LANES])


def _mla_up(proj, g_cq, g_ckv, wq, wk, wv, gk, cos, sin, *, batch, seq, n_eff, shift):
    m = proj.shape[0]
    ql, kvl = wq.shape[0], wk.shape[0]
    tm = PROJ_TM
    nblk = seq // tm
    nk, nv = wk.shape[1] // LANES, wv.shape[1] // LANES
    return pl.pallas_call(
        functools.partial(_mla_up_kernel, n_eff=n_eff, shift=shift),
        grid=(m // tm,),
        in_specs=[pl.BlockSpec((tm, ql), lambda i: (i, 0)),
                  pl.BlockSpec((tm, kvl), lambda i: (i, ql // kvl)),
                  pl.BlockSpec((tm, LANES), lambda i: (i, (ql + kvl) // LANES)),
                  pl.BlockSpec((1, ql), lambda i: (0, 0)),
                  pl.BlockSpec((1, kvl), lambda i: (0, 0)),
                  pl.BlockSpec(wq.shape, lambda i: (0, 0)),
                  pl.BlockSpec(wk.shape, lambda i: (0, 0)),
                  pl.BlockSpec(wv.shape, lambda i: (0, 0)),
                  pl.BlockSpec((1, LANES), lambda i: (0, 0)),
                  pl.BlockSpec((tm, LANES), lambda i: (i % nblk, 0)),
                  pl.BlockSpec((tm, LANES), lambda i: (i % nblk, 0))],
        out_specs=[pl.BlockSpec((tm, wq.shape[1]), lambda i: (i, 0)),
                   pl.BlockSpec((1, nk, tm, LANES), lambda i: (i // nblk, 0, i % nblk, 0)),
                   pl.BlockSpec((1, nv, 2 * VT_HALF, tm),
                                lambda i: (i // nblk, 0, 0, i % nblk))],
        out_shape=[jax.ShapeDtypeStruct((m, wq.shape[1]), F32),
                   jax.ShapeDtypeStruct((batch, nk, seq, LANES), BF16),
                   jax.ShapeDtypeStruct((batch, nv, 2 * VT_HALF, seq), BF16)],
        compiler_params=_params("parallel"),
        name="mla_up",
    )(proj, proj, proj, g_cq.reshape(1, ql), g_ckv.reshape(1, kvl), wq, wk, wv, gk, cos, sin)


def _attn_kernel(qe_ref, qo_ref, gq_ref, cos_ref, sin_ref, ke_ref, ko_ref, vt_ref,
                 coef_ref, gpost_ref, o_ref, s0_scr, s1_scr, qt_scr, res_scr,
                 *, n_groups, n_eff, shift, qscale, split_q, combine, post_norm):
    seq = vt_ref.shape[3]
    tq = ATTN_TQ
    nq = seq // tq
    rv = res_scr.shape[1]
    q_refs, k_refs = (qe_ref, qo_ref), (ke_ref, ko_ref)
    s_scrs = (s0_scr, s1_scr)
    lane = _lane_iota((tq, LANES))

    def block_rows(i):
        if isinstance(i, int):
            return pl.ds(i * tq, tq)
        return pl.ds(pl.multiple_of(i * tq, tq), tq)

    def fold8(x, op):
        part = op(x.reshape(FOLD_WAYS, x.shape[0] // (8 * FOLD_WAYS), 8, x.shape[1]), axis=1)
        return op(part, axis=0)

    def tree(parts, op):
        while len(parts) > 1:
            parts = [op(a, b) for a, b in zip(parts[::2], parts[1::2])] + parts[len(parts) & ~1:]
        return parts[0]

    def prep_q(i, u):
        rows = block_rows(i)
        q = _group_norm(q_refs[u][rows, :], gq_ref[...], n_groups, n_eff)
        q = _rotate(q, cos_ref[rows, :], sin_ref[rows, :], shift) * qscale
        if split_q:
            q = jnp.where((lane < D_HEAD) == (u == 0), q, 0.0)
        qt_scr[u] = q.T.astype(BF16)

    def tick(scores=None, softmax=None):
        acc = None
        maxes = []
        if scores is not None:
            qt = qt_scr[scores]
        for c0 in range(0, seq, ATTN_TK):
            ks = slice(c0, c0 + ATTN_TK)
            if scores is not None:
                for k0 in range(c0, c0 + ATTN_TK, ATTN_TKQ[combine]):
                    kq = slice(k0, k0 + ATTN_TKQ[combine])
                    s = jnp.dot(k_refs[scores][0, 0, kq, :], qt, preferred_element_type=F32)
                    s_scrs[scores][kq, :] = s
                    maxes.append(fold8(s, jnp.max))
            if softmax is not None:
                u, m = softmax
                p = jnp.exp2(s_scrs[u][ks, :] - m).astype(BF16)
                vrows = (slice(u * VT_HALF, (u + 1) * VT_HALF)
                         if combine == "select" else slice(0, VT_HALF + D_HEAD))
                pv = jnp.dot(vt_ref[0, 0, vrows, ks], p, preferred_element_type=F32)
                acc = pv if acc is None else acc + pv
        m_new = jnp.max(tree(maxes, jnp.maximum), axis=0, keepdims=True) if maxes else None
        res = None
        if acc is not None:
            inv_l = 1.0 / acc[D_HEAD:D_HEAD + 1]
            halves = [acc[h:h + D_HEAD] for h in range(0, acc.shape[0], VT_HALF)]
            res = jnp.concatenate(halves, axis=0) * inv_l
        return m_new, res

    def emit(i, res_e, res_o):
        if combine == "select":
            out_t = jnp.concatenate([res_e, res_o], axis=0)
        else:
            out_t = res_e + coef_ref[0:1, :] * res_o
        if post_norm:
            ms = jnp.mean(out_t * out_t, axis=0, keepdims=True)
            out_t = out_t * lax.rsqrt(ms + RMS_EPS)
        o_ref[block_rows(i), :] = (out_t.T * gpost_ref[...]).astype(o_ref.dtype)

    prep_q(0, 0)
    prep_q(0, 1)
    m_e, _ = tick(scores=0)
    prep_q(min(1, nq - 1), 0)
    m_o, res_e = tick(scores=1, softmax=(0, m_e))
    res_scr[0] = res_e
    prep_q(min(1, nq - 1), 1)

    def body(i, m_o):
        m_e_next, res_o = tick(scores=0, softmax=(1, m_o))
        ahead = jnp.minimum(i + 2, nq - 1)
        prep_q(ahead, 0)
        emit(i, res_scr[0], res_o)
        m_o_next, res_e = tick(scores=1, softmax=(0, m_e_next))
        prep_q(ahead, 1)
        res_scr[0] = res_e
        return m_o_next

    m_o = lax.fori_loop(0, nq - 1, body, m_o)
    _, res_o = tick(softmax=(1, m_o))
    emit(nq - 1, res_scr[0], res_o)


def _attention(q_src, q_blk, kn, k_idx, vt, v_idx, gq, cos, sin, coef, gpost, *,
               batch, seq, n_pairs, n_groups, n_eff, shift, scale, split_q, combine,
               post_norm):
    tq = ATTN_TQ
    m = batch * seq

    def qmap(which):
        return lambda b, p: (b, q_blk(p)[which])

    def kmap(which):
        return lambda b, p: (b, k_idx(p)[which], 0, 0)

    kern = functools.partial(_attn_kernel, n_groups=n_groups, n_eff=n_eff, shift=shift,
                             qscale=scale * math.log2(math.e), split_q=split_q,
                             combine=combine, post_norm=post_norm)
    return pl.pallas_call(
        kern,
        grid=(batch, n_pairs),
        in_specs=[pl.BlockSpec((seq, LANES), qmap(0)),
                  pl.BlockSpec((seq, LANES), qmap(1)),
                  pl.BlockSpec((1, LANES), lambda b, p: (0, 0)),
                  pl.BlockSpec((seq, LANES), lambda b, p: (0, 0),
                               pipeline_mode=pl.Buffered(1)),
                  pl.BlockSpec((seq, LANES), lambda b, p: (0, 0),
                               pipeline_mode=pl.Buffered(1)),
                  pl.BlockSpec((1, 1, seq, LANES), kmap(0)),
                  pl.BlockSpec((1, 1, seq, LANES), kmap(1)),
                  pl.BlockSpec((1, 1, 2 * VT_HALF, seq), lambda b, p: (b, v_idx(p), 0, 0)),
                  pl.BlockSpec((8, tq), lambda b, p: (0, 0)),
                  pl.BlockSpec((1, LANES), lambda b, p: (0, 0))],
        out_specs=pl.BlockSpec((seq, LANES), lambda b, p: (b, p)),
        out_shape=jax.ShapeDtypeStruct((m, n_pairs * LANES), BF16),
        scratch_shapes=[pltpu.VMEM((seq, tq), F32), pltpu.VMEM((seq, tq), F32),
                        pltpu.VMEM((2, LANES, tq), BF16),
                        pltpu.VMEM((1, D_HEAD if combine == "select" else LANES, tq), F32)],
        compiler_params=_params("parallel", "parallel"),
        name="attention",
    )(q_src, q_src, gq, cos, sin, kn, kn, vt, coef, gpost)


def _natten_kernel(q_ref, k_ref, v_ref, bias_ref, o_ref, q_scr, v_scr, *, rows):
    seq = v_ref.shape[0]
    nwin = WIN_R * GRID_W
    for r0 in range(0, seq, PREP_ROWS):
        sl = slice(r0, r0 + PREP_ROWS)
        q = q_ref[0, 0, sl, :]
        lo = _lane_iota(q.shape) < D_HEAD
        zero = jnp.zeros_like(q)
        q_scr[0, sl, :] = jnp.where(lo, q, zero)
        q_scr[1, sl, :] = jnp.where(lo, zero, q)
        v_scr[sl, :LANES] = v_ref[sl, :].astype(BF16)
        v_scr[sl, LANES:] = jnp.ones((PREP_ROWS, LANES), BF16)

    lane_lo = _lane_iota((GRID_W, LANES)) < D_HEAD

    def body(r, carry):
        w0 = jnp.clip(r - WIN_R // 2, 0, rows - WIN_R)
        var = r - w0
        kbase = pl.multiple_of(w0 * GRID_W, GRID_W)
        qbase = pl.multiple_of(r * GRID_W, GRID_W)
        kw = k_ref[0, 0, pl.ds(kbase, nwin), :]
        vw = v_scr[pl.ds(kbase, nwin), :]
        qrows = pl.ds(qbase, GRID_W)
        q2 = jnp.concatenate([q_scr[0, qrows, :], q_scr[1, qrows, :]], axis=0)
        s = lax.dot_general(q2, kw, (((1,), (1,)), ((), ())),
                            preferred_element_type=F32)
        z = s + bias_ref[0, var]
        zmax = jnp.max(z, axis=-1, keepdims=True)
        p = jnp.exp2(z - zmax).astype(BF16)
        acc = jnp.dot(p, vw, preferred_element_type=F32)
        res = acc[:, :LANES] / acc[:, LANES:]
        o_ref[qrows, :] = jnp.where(lane_lo, res[:GRID_W], res[GRID_W:]).astype(o_ref.dtype)
        return carry

    lax.fori_loop(0, rows, body, 0, unroll=16)


def _natten(tiles, proj, v_blk0, bias, batch, seq):
    rows = seq // GRID_W
    n_pairs = bias.shape[0]
    nwin = WIN_R * GRID_W
    return pl.pallas_call(
        functools.partial(_natten_kernel, rows=rows),
        grid=(batch, n_pairs),
        in_specs=[pl.BlockSpec((1, 1, seq, LANES), lambda b, p: (b, p, 0, 0)),
                  pl.BlockSpec((1, 1, seq, LANES), lambda b, p: (b, n_pairs + p, 0, 0)),
                  pl.BlockSpec((seq, LANES), lambda b, p: (b, v_blk0 + p)),
                  pl.BlockSpec((1, WIN_R, 2 * GRID_W, nwin), lambda b, p: (p, 0, 0, 0))],
        out_specs=pl.BlockSpec((seq, LANES), lambda b, p: (b, p)),
        out_shape=jax.ShapeDtypeStruct((batch * seq, n_pairs * LANES), BF16),
        scratch_shapes=[pltpu.VMEM((2, seq, LANES), BF16),
                        pltpu.VMEM((seq, 2 * LANES), BF16)],
        compiler_params=_params("parallel", "parallel"),
        name="natten",
    )(tiles, tiles, proj, bias)


def _natten_bias(rpb):
    col = np.arange(GRID_W)
    c0 = np.clip(col - WIN_C // 2, 0, GRID_W - WIN_C)
    ci = np.arange(GRID_W)
    inside = (ci[None, :] >= c0[:, None]) & (ci[None, :] < c0[:, None] + WIN_C)
    cb = ci[None, :] - col[:, None] + (WIN_C - 1)
    onehot = (cb[:, :, None] == np.arange(2 * WIN_C - 1)) & inside[:, :, None]
    toep = jnp.einsum("hrb,cjb->hrcj", rpb.astype(F32), jnp.asarray(onehot, F32),
                      precision=lax.Precision.HIGHEST)
    toep = jnp.where(inside[None, None], toep * math.log2(math.e), NEG_BIAS)
    n_heads = rpb.shape[0]
    variants = []
    for v in range(WIN_R):
        rows = toep[:, WIN_R - 1 - v:2 * WIN_R - 1 - v]
        variants.append(rows.transpose(0, 2, 1, 3).reshape(n_heads // 2, 2 * GRID_W,
                                                           WIN_R * GRID_W))
    return jnp.stack(variants, axis=1)


def _rot_tables(parts):
    ang = 0.0
    sign, off = [], 0
    for part in parts:
        if isinstance(part, int):
            sign.append(np.zeros(part, np.float32))
            off += part
            continue
        pos, dim, theta = part
        inv = jnp.power(theta, -jnp.arange(0, dim, 2, dtype=F32) / dim)
        inv_lanes = jnp.pad(jnp.concatenate([inv, inv]), (off, LANES - off - dim))
        ang = ang + pos[:, None] * inv_lanes[None, :]
        sign.append(np.repeat(np.float32([-1.0, 1.0]), dim // 2))
        off += dim
    return jnp.cos(ang), jnp.sin(ang) * jnp.asarray(np.concatenate(sign))[None, :]


def _dup(v):
    return jnp.concatenate([v, v]).reshape(1, -1).astype(F32)


def _pad_lanes(v):
    return jnp.pad(v.astype(F32), (0, LANES - v.shape[0])).reshape(1, LANES)


def kernel(x, norm_e, w_in_e, gq_a, gk_a, rpb_a, gq_b, gk_b, w_out_e, norm_o, w_in_o, g_cq, w_cq_b, g_ckv, w_ckv_b, gq_c, gk_c, gq_d, gk_d, lam_q1, lam_k1, lam_q2, lam_k2, g_sub_d, w_out_o):
    batch, seq, d = x.shape
    m = batch * seq
    x2 = x.reshape(m, d)
    t = jnp.arange(seq)
    pos = t.astype(F32)
    row = (t // GRID_W).astype(F32)
    col = (t % GRID_W).astype(F32)
    half = D_HEAD // 2
    ones_row = jnp.ones((1, LANES), F32)
    unit_coef = jnp.ones((8, ATTN_TQ), F32)

    n_a = 8 * D_HEAD
    base_b = 3 * n_a
    kb0 = base_b + 8 * D_HEAD
    vb0 = kb0 + 2 * D_HEAD
    gate0 = vb0 + 2 * D_HEAD
    w0 = w_in_e[0]
    dup_heads = [w0[:, c:c + D_HEAD] for c in (kb0, kb0, kb0 + D_HEAD, kb0 + D_HEAD,
                                                 vb0, vb0, vb0 + D_HEAD, vb0 + D_HEAD)]
    w_e = jnp.concatenate([w0[:, :kb0]] + dup_heads + [w0[:, gate0:]],
                          axis=1).astype(BF16)
    axial = [(row, half, AXIAL_THETA), (col, half, AXIAL_THETA)]
    cos_b, sin_b = _rot_tables(axial + axial)
    gains_e = jnp.concatenate([_dup(gq_a[0]) * (D_HEAD ** -0.5 * math.log2(math.e)),
                               _dup(gk_a[0]), _dup(gk_b[0])], axis=0)
    tiles_e = (tuple((t, 0, False) for t in range(0, 4))
               + tuple((t, 1, False) for t in range(4, 8)) + ((16, 2, True), (17, 2, True)))
    proj_e, gate_e, kt_e, vp_b = _proj_in(
        x2, norm_e[0], w_e, gains_e, cos_b, sin_b, batch=batch, seq=seq, n_gate=d,
        k_tiles=tiles_e, v_tiles=(18, 19), n_groups=2, n_eff=D_HEAD, shift=half // 2)
    bias = _natten_bias(rpb_a[0])
    oa = _natten(kt_e, proj_e, 8, bias, batch, seq)
    ob = _attention(proj_e, lambda p: (12 + p, 12 + p), kt_e,
                    lambda p: (8 + p // 2, 8 + p // 2),
                    vp_b, lambda p: p // 2, _dup(gq_b[0]), cos_b, sin_b, unit_coef, ones_row,
                    batch=batch, seq=seq, n_pairs=4, n_groups=2, n_eff=D_HEAD,
                    shift=half // 2, scale=D_HEAD ** -0.5, split_q=True, combine="select",
                    post_norm=False)
    x2 = _proj_out(oa, ob, gate_e, w_out_e[0].astype(BF16), x2)

    w1 = w_in_o[0]
    q_lora, kv_lora = g_cq.shape[1], g_ckv.shape[1]
    o_kpe = q_lora + kv_lora
    o_qd = o_kpe + C_ROPE
    zeros = functools.partial(jnp.zeros, dtype=w1.dtype)
    w_o = jnp.concatenate([
        w1[:, :o_kpe],
        zeros((d, C_NOPE)), w1[:, o_kpe:o_qd], zeros((d, LANES - C_NOPE - C_ROPE)),
        w1[:, o_qd:]], axis=1).astype(BF16)
    partial = [(pos, ROT_DIM, ROPE_THETA), D_HEAD - ROT_DIM]
    cos_d, sin_d = _rot_tables(partial + partial)
    proj_o, gate_o, kt_d, vp_d = _proj_in(
        x2, norm_o[0], w_o, _dup(gk_d[0]), cos_d, sin_d, batch=batch, seq=seq, n_gate=d,
        k_tiles=tuple((t, 0, True) for t in range(8, 12)), v_tiles=(12, 13, 14, 15),
        n_groups=2, n_eff=D_HEAD, shift=ROT_DIM // 2)
    n_hc = 8
    dqc = C_NOPE + C_ROPE
    wq = jnp.pad(w_cq_b[0].reshape(q_lora, n_hc, dqc),
                 ((0, 0), (0, 0), (0, LANES - dqc))).reshape(q_lora, n_hc * LANES).astype(BF16)
    wkv = w_ckv_b[0].reshape(kv_lora, n_hc, C_NOPE + C_V)
    wk = jnp.pad(wkv[:, :, :C_NOPE],
                 ((0, 0), (0, 0), (0, LANES - C_NOPE))).reshape(kv_lora, n_hc * LANES).astype(BF16)
    wv = wkv[:, :, C_NOPE:].reshape(kv_lora, n_hc * C_V).astype(BF16)
    cos_c, sin_c = _rot_tables([C_NOPE, (pos, C_ROPE, MLA_THETA), LANES - C_NOPE - C_ROPE])
    q_c, kt_c, vp_c = _mla_up(proj_o, g_cq[0], g_ckv[0], wq, wk, wv, _pad_lanes(gk_c[0]),
                              cos_c, sin_c, batch=batch, seq=seq, n_eff=dqc,
                              shift=C_ROPE // 2)
    oc = _attention(q_c, lambda p: (2 * p, 2 * p + 1), kt_c, lambda p: (2 * p, 2 * p + 1),
                    vp_c, lambda p: p, _pad_lanes(gq_c[0]), cos_c, sin_c, unit_coef, ones_row,
                    batch=batch, seq=seq, n_pairs=4, n_groups=1, n_eff=dqc,
                    shift=C_ROPE // 2, scale=dqc ** -0.5, split_q=False, combine="select",
                    post_norm=False)

    lam_init = 0.8 - 0.6 * math.exp(-0.3 * 1)
    lam = (jnp.exp(jnp.sum(lam_q1[0].astype(F32) * lam_k1[0].astype(F32)))
           - jnp.exp(jnp.sum(lam_q2[0].astype(F32) * lam_k2[0].astype(F32))) + lam_init)
    diff_coef = jnp.full((8, ATTN_TQ), -lam, F32)
    gpost = (g_sub_d[0].astype(F32) * (1.0 - lam_init)).reshape(1, LANES)
    od = _attention(proj_o, lambda p: (4 + p, 4 + p), kt_d, lambda p: (p, p),
                    vp_d, lambda p: p, _dup(gq_d[0]), cos_d, sin_d, diff_coef, gpost,
                    batch=batch, seq=seq, n_pairs=4, n_groups=2, n_eff=D_HEAD,
                    shift=ROT_DIM // 2, scale=D_HEAD ** -0.5, split_q=True, combine="add",
                    post_norm=True)
    x2 = _proj_out(oc, od, gate_o, w_out_o[0].astype(BF16), x2)
    return x2.reshape(batch, seq, d)
```

```python
import functools
import math

import jax
import jax.numpy as jnp
import numpy as np
from jax import lax
from jax.experimental import pallas as pl
from jax.experimental.pallas import tpu as pltpu

F32 = jnp.float32
BF16 = jnp.bfloat16

LANES = 128
SUBLANES = 8
VMEM_LIMIT_BYTES = 56 * 1024 * 1024

D_HEAD = 64
GRID_W = 64
RMS_EPS = 1e-6
WIN_R = 8
WIN_C = 16
AXIAL_THETA = 10000.0
MLA_THETA = 10000.0
ROPE_THETA = 500000.0
C_NOPE = 64
C_ROPE = 32
C_V = 64
ROT_DIM = 16
NEG_BIAS = -1e30

PROJ_TM = 512
PROJ_TN = 512
ATTN_TQ = 256
ATTN_STEP_PAIRS = 1
ATTN_TK = 1024
ATTN_TKQ = {"select": 512, "add": 1024}
FOLD_WAYS = 4
PREP_ROWS = 512
ONES_ROWS = 16
VT_HALF = D_HEAD + ONES_ROWS


def _params(*sem):
    return pltpu.CompilerParams(dimension_semantics=sem,
                                vmem_limit_bytes=VMEM_LIMIT_BYTES)


def _lane_iota(shape):
    return lax.broadcasted_iota(jnp.int32, shape, len(shape) - 1)


def _group_norm(x, gain, n_groups, n_eff):
    y = x * x
    tot = jnp.sum(y, axis=-1, keepdims=True)
    if n_groups == 1:
        ms = tot * (1.0 / n_eff)
    else:
        lo_mask = _lane_iota(x.shape) < D_HEAD
        lo = jnp.sum(jnp.where(lo_mask, y, 0.0), axis=-1, keepdims=True)
        ms = jnp.where(lo_mask, lo, tot - lo) * (1.0 / D_HEAD)
    return x * lax.rsqrt(ms + RMS_EPS) * gain


def _rotate(x, cos, sin_signed, shift):
    n = x.shape[-1]
    up = pltpu.roll(x, n - shift, 1)
    down = pltpu.roll(x, shift, 1)
    partner = jnp.where((_lane_iota(x.shape) & shift) == 0, up, down)
    return x * cos + partner * sin_signed


def _store_keys(kn_ref, j, tile, gain, cos, sin, *, n_groups, n_eff, shift, rotate=True):
    k = _group_norm(tile, gain, n_groups, n_eff)
    if rotate:
        k = _rotate(k, cos, sin, shift)
    kn_ref[0, j] = k.astype(BF16)


def _store_values_t(vt_ref, j, tile):
    vt = tile.T.astype(BF16)
    ones = jnp.ones((ONES_ROWS, vt.shape[1]), BF16)
    for h in range(2):
        base = h * VT_HALF
        vt_ref[0, j, base:base + D_HEAD, :] = vt[h * D_HEAD:(h + 1) * D_HEAD]
        vt_ref[0, j, base + D_HEAD:base + VT_HALF, :] = ones


def _latent_up(cq, ckv, kpe, gq_ref, gkv_ref, wq_ref, wk_ref, wv_ref, gk_ref, cos_ref, sin_ref,
               q_ref, kn_ref, vt_ref, *, n_eff, shift):
    def normed(x, g):
        ms = jnp.mean(x * x, axis=-1, keepdims=True)
        return (x * lax.rsqrt(ms + RMS_EPS) * g).astype(BF16)

    hq = normed(cq, gq_ref[...])
    hkv = normed(ckv, gkv_ref[...])
    q_ref[...] = jnp.dot(hq, wq_ref[...], preferred_element_type=F32)
    kk = jnp.dot(hkv, wk_ref[...], preferred_element_type=F32)
    gk = gk_ref[...]
    pe_rot = _rotate(kpe * gk, cos_ref[...], sin_ref[...], shift)
    ss_pe = jnp.sum(kpe * kpe, axis=-1, keepdims=True)
    for h in range(kn_ref.shape[1]):
        nope = kk[:, h * LANES:(h + 1) * LANES]
        ss = jnp.sum(nope * nope, axis=-1, keepdims=True) + ss_pe
        r = lax.rsqrt(ss * (1.0 / n_eff) + RMS_EPS)
        kn_ref[0, h] = ((nope * gk + pe_rot) * r).astype(BF16)
    vv = jnp.dot(hkv, wv_ref[...], preferred_element_type=F32)
    for j in range(vt_ref.shape[1]):
        _store_values_t(vt_ref, j, vv[:, j * LANES:(j + 1) * LANES])


N_LATENT_IN = 8


def _proj_in_kernel(*refs, k_tiles, v_tiles, n_groups, n_eff, shift, latent):
    n_in = 6 + (N_LATENT_IN if latent else 0)
    x_ref, g_ref, w_ref, gk_ref, cos_ref, sin_ref = refs[:6]
    o_ref, gate_ref, kn_ref, vt_ref = refs[n_in:n_in + 4]
    x = x_ref[...]
    ms = jnp.mean(x * x, axis=-1, keepdims=True)
    h = (x * lax.rsqrt(ms + RMS_EPS) * g_ref[...]).astype(BF16)
    n_mix = o_ref.shape[1]
    for n0 in range(0, w_ref.shape[1], PROJ_TN):
        y = jnp.dot(h, w_ref[:, n0:n0 + PROJ_TN], preferred_element_type=F32)
        if n0 >= n_mix:
            gate_ref[:, n0 - n_mix:n0 - n_mix + PROJ_TN] = y.astype(BF16)
            continue
        o_ref[:, n0:n0 + PROJ_TN] = y
        if latent and n0 == 0:
            ql, kvl, lat_eff, lat_shift = latent
            _latent_up(o_ref[:, :ql], o_ref[:, ql:ql + kvl], o_ref[:, ql + kvl:ql + kvl + LANES],
                       *refs[6:n_in], *refs[n_in + 4:], n_eff=lat_eff, shift=lat_shift)
        for j, (t, g, rotate) in enumerate(k_tiles):
            if n0 <= t * LANES < n0 + PROJ_TN:
                _store_keys(kn_ref, j, o_ref[:, t * LANES:(t + 1) * LANES],
                            gk_ref[g * SUBLANES:g * SUBLANES + 1, :],
                            cos_ref[...], sin_ref[...], n_groups=n_groups, n_eff=n_eff,
                            shift=shift, rotate=rotate)
        for j, t in enumerate(v_tiles):
            if n0 <= t * LANES < n0 + PROJ_TN:
                _store_values_t(vt_ref, j, o_ref[:, t * LANES:(t + 1) * LANES])


def _proj_in(x2, gain, w, gk, cos, sin, *, batch, seq, n_gate, k_tiles, v_tiles, n_groups,
             n_eff, shift, latent=None):
    m, d = x2.shape
    n = w.shape[1]
    n_mix = n - n_gate
    tm = PROJ_TM
    nblk = seq // tm
    nk, nv = len(k_tiles), len(v_tiles)
    const = lambda i: (0, 0)
    rows = lambda i: (i % nblk, 0)
    tiles = lambda i: (i // nblk, 0, i % nblk, 0)
    tiles_t = lambda i: (i // nblk, 0, 0, i % nblk)
    operands = [x2, gain.reshape(1, d), w, jnp.repeat(gk, SUBLANES, axis=0), cos, sin]
    in_specs = [pl.BlockSpec((tm, d), lambda i: (i, 0)),
                pl.BlockSpec((1, d), const),
                pl.BlockSpec((d, n), const),
                pl.BlockSpec((gk.shape[0] * SUBLANES, LANES), const),
                pl.BlockSpec((tm, LANES), rows),
                pl.BlockSpec((tm, LANES), rows)]
    out_specs = [pl.BlockSpec((tm, n_mix), lambda i: (i, 0)),
                 pl.BlockSpec((tm, n_gate), lambda i: (i, 0)),
                 pl.BlockSpec((1, nk, tm, LANES), tiles),
                 pl.BlockSpec((1, nv, 2 * VT_HALF, tm), tiles_t)]
    out_shape = [jax.ShapeDtypeStruct((m, n_mix), F32),
                 jax.ShapeDtypeStruct((m, n_gate), BF16),
                 jax.ShapeDtypeStruct((batch, nk, seq, LANES), BF16),
                 jax.ShapeDtypeStruct((batch, nv, 2 * VT_HALF, seq), BF16)]
    lat_static = None
    if latent is not None:
        g_cq, g_ckv, wq, wk, wv, gk_l, cos_l, sin_l, lat_eff, lat_shift = latent
        ql, kvl = wq.shape[0], wk.shape[0]
        assert ql + kvl + LANES <= PROJ_TN
        nkl, nvl = wk.shape[1] // LANES, wv.shape[1] // LANES
        lat_static = (ql, kvl, lat_eff, lat_shift)
        operands += [g_cq.reshape(1, ql), g_ckv.reshape(1, kvl), wq, wk, wv, gk_l, cos_l, sin_l]
        in_specs += [pl.BlockSpec((1, ql), const), pl.BlockSpec((1, kvl), const),
                     pl.BlockSpec(wq.shape, const), pl.BlockSpec(wk.shape, const),
                     pl.BlockSpec(wv.shape, const), pl.BlockSpec((1, LANES), const),
                     pl.BlockSpec((tm, LANES), rows), pl.BlockSpec((tm, LANES), rows)]
        assert len(operands) == 6 + N_LATENT_IN
        out_specs += [pl.BlockSpec((tm, wq.shape[1]), lambda i: (i, 0)),
                      pl.BlockSpec((1, nkl, tm, LANES), tiles),
                      pl.BlockSpec((1, nvl, 2 * VT_HALF, tm), tiles_t)]
        out_shape += [jax.ShapeDtypeStruct((m, wq.shape[1]), F32),
                      jax.ShapeDtypeStruct((batch, nkl, seq, LANES), BF16),
                      jax.ShapeDtypeStruct((batch, nvl, 2 * VT_HALF, seq), BF16)]
    kern = functools.partial(_proj_in_kernel, k_tiles=k_tiles, v_tiles=v_tiles,
                             n_groups=n_groups, n_eff=n_eff, shift=shift, latent=lat_static)
    return pl.pallas_call(
        kern,
        grid=(m // tm,),
        in_specs=in_specs,
        out_specs=out_specs,
        out_shape=out_shape,
        compiler_params=_params("parallel"),
        name="proj_in",
    )(*operands)


def _proj_out_kernel(oa_ref, ob_ref, ga_ref, gb_ref, w_ref, x_ref, o_ref):
    def gated(o_r, g_r):
        g = g_r[...].astype(F32)
        return (o_r[...].astype(F32) * (g / (1.0 + jnp.exp(-g)))).astype(BF16)

    half = oa_ref.shape[1]
    y = jnp.dot(gated(oa_ref, ga_ref), w_ref[:half, :], preferred_element_type=F32)
    y = y + jnp.dot(gated(ob_ref, gb_ref), w_ref[half:, :], preferred_element_type=F32)
    o_ref[...] = x_ref[...] + y


def _proj_out(oa, ob, gate, w, x2):
    m, d = x2.shape
    half = oa.shape[1]
    return pl.pallas_call(
        _proj_out_kernel,
        grid=(m // PROJ_TM,),
        in_specs=[pl.BlockSpec((PROJ_TM, half), lambda i: (i, 0)),
                  pl.BlockSpec((PROJ_TM, half), lambda i: (i, 0)),
                  pl.BlockSpec((PROJ_TM, half), lambda i: (i, 0)),
                  pl.BlockSpec((PROJ_TM, half), lambda i: (i, 1)),
                  pl.BlockSpec((2 * half, d), lambda i: (0, 0)),
                  pl.BlockSpec((PROJ_TM, d), lambda i: (i, 0))],
        out_specs=pl.BlockSpec((PROJ_TM, d), lambda i: (i, 0)),
        out_shape=jax.ShapeDtypeStruct((m, d), F32),
        compiler_params=_params("parallel"),
        name="proj_out",
    )(oa, ob, gate, gate, w, x2)


def _attn_kernel(*refs, n_step_pairs, n_groups, n_eff, shift, qscale, split_q, combine,
                 post_norm):
    g2 = 2 * n_step_pairs
    q_refs = [refs[2 * pp:2 * pp + 2] for pp in range(n_step_pairs)]
    gq_ref, cos_ref, sin_ref = refs[g2:g2 + 3]
    k_refs = [refs[g2 + 3 + 2 * pp:g2 + 5 + 2 * pp] for pp in range(n_step_pairs)]
    vt_refs = refs[2 * g2 + 3:2 * g2 + 3 + n_step_pairs]
    coef_ref, gpost_ref, o_ref, s0_scr, s1_scr, qt_scr, res_scr = refs[2 * g2 + 3 + n_step_pairs:]
    seq = vt_refs[0].shape[3]
    tq = ATTN_TQ
    nq = seq // tq
    s_scrs = (s0_scr, s1_scr)
    lane = _lane_iota((tq, LANES))

    def block_rows(i):
        if isinstance(i, int):
            return pl.ds(i * tq, tq)
        return pl.ds(pl.multiple_of(i * tq, tq), tq)

    def fold8(x, op):
        part = op(x.reshape(FOLD_WAYS, x.shape[0] // (8 * FOLD_WAYS), 8, x.shape[1]), axis=1)
        return op(part, axis=0)

    def tree(parts, op):
        while len(parts) > 1:
            parts = [op(a, b) for a, b in zip(parts[::2], parts[1::2])] + parts[len(parts) & ~1:]
        return parts[0]

    def prep_q(pp, i, u):
        rows = block_rows(i)
        q = _group_norm(q_refs[pp][u][rows, :], gq_ref[...], n_groups, n_eff)
        q = _rotate(q, cos_ref[rows, :], sin_ref[rows, :], shift) * qscale
        if split_q:
            q = jnp.where((lane < D_HEAD) == (u == 0), q, 0.0)
        qt_scr[u] = q.T.astype(BF16)

    def tick(scores=None, softmax=None):
        acc = None
        maxes = []
        if scores is not None:
            sp, su = scores
            qt = qt_scr[su]
        for c0 in range(0, seq, ATTN_TK):
            ks = slice(c0, c0 + ATTN_TK)
            if scores is not None:
                for k0 in range(c0, c0 + ATTN_TK, ATTN_TKQ[combine]):
                    kq = slice(k0, k0 + ATTN_TKQ[combine])
                    s = jnp.dot(k_refs[sp][su][0, 0, kq, :], qt, preferred_element_type=F32)
                    s_scrs[su][kq, :] = s
                    maxes.append(fold8(s, jnp.max))
            if softmax is not None:
                pp, u, m = softmax
                p = jnp.exp2(s_scrs[u][ks, :] - m).astype(BF16)
                vrows = (slice(u * VT_HALF, (u + 1) * VT_HALF)
                         if combine == "select" else slice(0, VT_HALF + D_HEAD))
                pv = jnp.dot(vt_refs[pp][0, 0, vrows, ks], p, preferred_element_type=F32)
                acc = pv if acc is None else acc + pv
        m_new = jnp.max(tree(maxes, jnp.maximum), axis=0, keepdims=True) if maxes else None
        res = None
        if acc is not None:
            inv_l = 1.0 / acc[D_HEAD:D_HEAD + 1]
            halves = [acc[h:h + D_HEAD] for h in range(0, acc.shape[0], VT_HALF)]
            res = jnp.concatenate(halves, axis=0) * inv_l
        return m_new, res

    def emit(pp, i, res_e, res_o):
        if combine == "select":
            out_t = jnp.concatenate([res_e, res_o], axis=0)
        else:
            out_t = res_e + coef_ref[0:1, :] * res_o
        if post_norm:
            ms = jnp.mean(out_t * out_t, axis=0, keepdims=True)
            out_t = out_t * lax.rsqrt(ms + RMS_EPS)
        o_ref[block_rows(i), pp * LANES:(pp + 1) * LANES] = (
            out_t.T * gpost_ref[...]).astype(o_ref.dtype)

    nxt = min(1, nq - 1)
    prep_q(0, 0, 0)
    prep_q(0, 0, 1)
    m_o = res_e = None
    for pp in range(n_step_pairs):
        prev = None if pp == 0 else (pp - 1, 1, m_o)
        m_e, res_o = tick(scores=(pp, 0), softmax=prev)
        prep_q(pp, nxt, 0)
        if prev is not None:
            emit(pp - 1, nq - 1, res_scr[0], res_o)
        m_o, res_e = tick(scores=(pp, 1), softmax=(pp, 0, m_e))
        res_scr[0] = res_e
        prep_q(pp, nxt, 1)

        def body(i, m_o, pp=pp):
            m_e_next, res_o = tick(scores=(pp, 0), softmax=(pp, 1, m_o))
            ahead = jnp.minimum(i + 2, nq - 1)
            prep_q(pp, ahead, 0)
            emit(pp, i, res_scr[0], res_o)
            m_o_next, res_e = tick(scores=(pp, 1), softmax=(pp, 0, m_e_next))
            prep_q(pp, ahead, 1)
            res_scr[0] = res_e
            return m_o_next

        m_o = lax.fori_loop(0, nq - 1, body, m_o)
        if pp + 1 < n_step_pairs:
            prep_q(pp + 1, 0, 0)
            prep_q(pp + 1, 0, 1)
    _, res_o = tick(softmax=(n_step_pairs - 1, 1, m_o))
    emit(n_step_pairs - 1, nq - 1, res_scr[0], res_o)


def _attention(q_src, q_blk, kn, k_idx, vt, v_idx, gq, cos, sin, coef, gpost, *,
               batch, seq, n_pairs, n_groups, n_eff, shift, scale, split_q, combine,
               post_norm):
    tq = ATTN_TQ
    m = batch * seq
    g = ATTN_STEP_PAIRS
    const = lambda b, s: (0, 0)

    q_specs = [pl.BlockSpec((seq, LANES), lambda b, s, pp=pp, w=w: (b, q_blk(s * g + pp)[w]))
               for pp in range(g) for w in range(2)]
    k_specs = [pl.BlockSpec((1, 1, seq, LANES),
                            lambda b, s, pp=pp, w=w: (b, k_idx(s * g + pp)[w], 0, 0))
               for pp in range(g) for w in range(2)]
    v_specs = [pl.BlockSpec((1, 1, 2 * VT_HALF, seq),
                            lambda b, s, pp=pp: (b, v_idx(s * g + pp), 0, 0))
               for pp in range(g)]
    table = pl.BlockSpec((seq, LANES), const, pipeline_mode=pl.Buffered(1))
    kern = functools.partial(_attn_kernel, n_step_pairs=g, n_groups=n_groups, n_eff=n_eff,
                             shift=shift, qscale=scale * math.log2(math.e), split_q=split_q,
                             combine=combine, post_norm=post_norm)
    return pl.pallas_call(
        kern,
        grid=(batch, n_pairs // g),
        in_specs=(q_specs + [pl.BlockSpec((1, LANES), const), table, table] + k_specs
                  + v_specs + [pl.BlockSpec((8, tq), const), pl.BlockSpec((1, LANES), const)]),
        out_specs=pl.BlockSpec((seq, g * LANES), lambda b, s: (b, s)),
        out_shape=jax.ShapeDtypeStruct((m, n_pairs * LANES), BF16),
        scratch_shapes=[pltpu.VMEM((seq, tq), F32), pltpu.VMEM((seq, tq), F32),
                        pltpu.VMEM((2, LANES, tq), BF16),
                        pltpu.VMEM((1, D_HEAD if combine == "select" else LANES, tq), F32)],
        compiler_params=_params("parallel", "parallel"),
        name="attention",
    )(*([q_src] * (2 * g)), gq, cos, sin, *([kn] * (2 * g)), *([vt] * g), coef, gpost)


def _natten_kernel(q_ref, k_ref, v_ref, bias_ref, o_ref, q_scr, v_scr, *, rows):
    seq = v_ref.shape[0]
    nwin = WIN_R * GRID_W
    for r0 in range(0, seq, PREP_ROWS):
        sl = slice(r0, r0 + PREP_ROWS)
        q = q_ref[0, 0, sl, :]
        lo = _lane_iota(q.shape) < D_HEAD
        zero = jnp.zeros_like(q)
        q_scr[0, sl, :] = jnp.where(lo, q, zero)
        q_scr[1, sl, :] = jnp.where(lo, zero, q)
        v_scr[sl, :LANES] = v_ref[sl, :].astype(BF16)
        v_scr[sl, LANES:] = jnp.ones((PREP_ROWS, LANES), BF16)

    lane_lo = _lane_iota((GRID_W, LANES)) < D_HEAD

    def body(r, carry):
        w0 = jnp.clip(r - WIN_R // 2, 0, rows - WIN_R)
        var = r - w0
        kbase = pl.multiple_of(w0 * GRID_W, GRID_W)
        qbase = pl.multiple_of(r * GRID_W, GRID_W)
        kw = k_ref[0, 0, pl.ds(kbase, nwin), :]
        vw = v_scr[pl.ds(kbase, nwin), :]
        qrows = pl.ds(qbase, GRID_W)
        q2 = jnp.concatenate([q_scr[0, qrows, :], q_scr[1, qrows, :]], axis=0)
        s = lax.dot_general(q2, kw, (((1,), (1,)), ((), ())),
                            preferred_element_type=F32)
        z = s + bias_ref[0, var]
        zmax = jnp.max(z, axis=-1, keepdims=True)
        p = jnp.exp2(z - zmax).astype(BF16)
        acc = jnp.dot(p, vw, preferred_element_type=F32)
        res = acc[:, :LANES] / acc[:, LANES:]
        o_ref[qrows, :] = jnp.where(lane_lo, res[:GRID_W], res[GRID_W:]).astype(o_ref.dtype)
        return carry

    lax.fori_loop(0, rows, body, 0, unroll=16)


def _natten(tiles, proj, v_blk0, bias, batch, seq):
    rows = seq // GRID_W
    n_pairs = bias.shape[0]
    nwin = WIN_R * GRID_W
    return pl.pallas_call(
        functools.partial(_natten_kernel, rows=rows),
        grid=(batch, n_pairs),
        in_specs=[pl.BlockSpec((1, 1, seq, LANES), lambda b, p: (b, p, 0, 0)),
                  pl.BlockSpec((1, 1, seq, LANES), lambda b, p: (b, n_pairs + p, 0, 0)),
                  pl.BlockSpec((seq, LANES), lambda b, p: (b, v_blk0 + p)),
                  pl.BlockSpec((1, WIN_R, 2 * GRID_W, nwin), lambda b, p: (p, 0, 0, 0))],
        out_specs=pl.BlockSpec((seq, LANES), lambda b, p: (b, p)),
        out_shape=jax.ShapeDtypeStruct((batch * seq, n_pairs * LANES), BF16),
        scratch_shapes=[pltpu.VMEM((2, seq, LANES), BF16),
                        pltpu.VMEM((seq, 2 * LANES), BF16)],
        compiler_params=_params("parallel", "parallel"),
        name="natten",
    )(tiles, tiles, proj, bias)


def _natten_bias(rpb):
    col = np.arange(GRID_W)
    c0 = np.clip(col - WIN_C // 2, 0, GRID_W - WIN_C)
    ci = np.arange(GRID_W)
    inside = (ci[None, :] >= c0[:, None]) & (ci[None, :] < c0[:, None] + WIN_C)
    cb = ci[None, :] - col[:, None] + (WIN_C - 1)
    onehot = (cb[:, :, None] == np.arange(2 * WIN_C - 1)) & inside[:, :, None]
    toep = jnp.einsum("hrb,cjb->hrcj", rpb.astype(F32), jnp.asarray(onehot, F32),
                      precision=lax.Precision.HIGHEST)
    toep = jnp.where(inside[None, None], toep * math.log2(math.e), NEG_BIAS)
    n_heads = rpb.shape[0]
    variants = []
    for v in range(WIN_R):
        rows = toep[:, WIN_R - 1 - v:2 * WIN_R - 1 - v]
        variants.append(rows.transpose(0, 2, 1, 3).reshape(n_heads // 2, 2 * GRID_W,
                                                           WIN_R * GRID_W))
    return jnp.stack(variants, axis=1)


def _rot_tables(parts):
    ang = 0.0
    sign, off = [], 0
    for part in parts:
        if isinstance(part, int):
            sign.append(np.zeros(part, np.float32))
            off += part
            continue
        pos, dim, theta = part
        inv = jnp.power(theta, -jnp.arange(0, dim, 2, dtype=F32) / dim)
        inv_lanes = jnp.pad(jnp.concatenate([inv, inv]), (off, LANES - off - dim))
        ang = ang + pos[:, None] * inv_lanes[None, :]
        sign.append(np.repeat(np.float32([-1.0, 1.0]), dim // 2))
        off += dim
    return jnp.cos(ang), jnp.sin(ang) * jnp.asarray(np.concatenate(sign))[None, :]


def _dup(v):
    return jnp.concatenate([v, v]).reshape(1, -1).astype(F32)


def _pad_lanes(v):
    return jnp.pad(v.astype(F32), (0, LANES - v.shape[0])).reshape(1, LANES)


def kernel(x, norm_e, w_in_e, gq_a, gk_a, rpb_a, gq_b, gk_b, w_out_e, norm_o, w_in_o, g_cq, w_cq_b, g_ckv, w_ckv_b, gq_c, gk_c, gq_d, gk_d, lam_q1, lam_k1, lam_q2, lam_k2, g_sub_d, w_out_o):
    batch, seq, d = x.shape
    m = batch * seq
    x2 = x.reshape(m, d)
    t = jnp.arange(seq)
    pos = t.astype(F32)
    row = (t // GRID_W).astype(F32)
    col = (t % GRID_W).astype(F32)
    half = D_HEAD // 2
    ones_row = jnp.ones((1, LANES), F32)
    unit_coef = jnp.ones((8, ATTN_TQ), F32)

    n_a = 8 * D_HEAD
    base_b = 3 * n_a
    kb0 = base_b + 8 * D_HEAD
    vb0 = kb0 + 2 * D_HEAD
    gate0 = vb0 + 2 * D_HEAD
    w0 = w_in_e[0]
    dup_heads = [w0[:, c:c + D_HEAD] for c in (kb0, kb0, kb0 + D_HEAD, kb0 + D_HEAD,
                                                 vb0, vb0, vb0 + D_HEAD, vb0 + D_HEAD)]
    w_e = jnp.concatenate([w0[:, :kb0]] + dup_heads + [w0[:, gate0:]],
                          axis=1).astype(BF16)
    axial = [(row, half, AXIAL_THETA), (col, half, AXIAL_THETA)]
    cos_b, sin_b = _rot_tables(axial + axial)
    gains_e = jnp.concatenate([_dup(gq_a[0]) * (D_HEAD ** -0.5 * math.log2(math.e)),
                               _dup(gk_a[0]), _dup(gk_b[0])], axis=0)
    tiles_e = (tuple((t, 0, False) for t in range(0, 4))
               + tuple((t, 1, False) for t in range(4, 8)) + ((16, 2, True), (17, 2, True)))
    proj_e, gate_e, kt_e, vp_b = _proj_in(
        x2, norm_e[0], w_e, gains_e, cos_b, sin_b, batch=batch, seq=seq, n_gate=d,
        k_tiles=tiles_e, v_tiles=(18, 19), n_groups=2, n_eff=D_HEAD, shift=half // 2)
    bias = _natten_bias(rpb_a[0])
    oa = _natten(kt_e, proj_e, 8, bias, batch, seq)
    ob = _attention(proj_e, lambda p: (12 + p, 12 + p), kt_e,
                    lambda p: (8 + p // 2, 8 + p // 2),
                    vp_b, lambda p: p // 2, _dup(gq_b[0]), cos_b, sin_b, unit_coef, ones_row,
                    batch=batch, seq=seq, n_pairs=4, n_groups=2, n_eff=D_HEAD,
                    shift=half // 2, scale=D_HEAD ** -0.5, split_q=True, combine="select",
                    post_norm=False)
    x2 = _proj_out(oa, ob, gate_e, w_out_e[0].astype(BF16), x2)

    w1 = w_in_o[0]
    q_lora, kv_lora = g_cq.shape[1], g_ckv.shape[1]
    o_kpe = q_lora + kv_lora
    o_qd = o_kpe + C_ROPE
    zeros = functools.partial(jnp.zeros, dtype=w1.dtype)
    w_o = jnp.concatenate([
        w1[:, :o_kpe],
        zeros((d, C_NOPE)), w1[:, o_kpe:o_qd], zeros((d, LANES - C_NOPE - C_ROPE)),
        w1[:, o_qd:]], axis=1).astype(BF16)
    partial = [(pos, ROT_DIM, ROPE_THETA), D_HEAD - ROT_DIM]
    cos_d, sin_d = _rot_tables(partial + partial)
    n_hc = 8
    dqc = C_NOPE + C_ROPE
    wq = jnp.pad(w_cq_b[0].reshape(q_lora, n_hc, dqc),
                 ((0, 0), (0, 0), (0, LANES - dqc))).reshape(q_lora, n_hc * LANES).astype(BF16)
    wkv = w_ckv_b[0].reshape(kv_lora, n_hc, C_NOPE + C_V)
    wk = jnp.pad(wkv[:, :, :C_NOPE],
                 ((0, 0), (0, 0), (0, LANES - C_NOPE))).reshape(kv_lora, n_hc * LANES).astype(BF16)
    wv = wkv[:, :, C_NOPE:].reshape(kv_lora, n_hc * C_V).astype(BF16)
    cos_c, sin_c = _rot_tables([C_NOPE, (pos, C_ROPE, MLA_THETA), LANES - C_NOPE - C_ROPE])
    proj_o, gate_o, kt_d, vp_d, q_c, kt_c, vp_c = _proj_in(
        x2, norm_o[0], w_o, _dup(gk_d[0]), cos_d, sin_d, batch=batch, seq=seq, n_gate=d,
        k_tiles=tuple((t, 0, True) for t in range(8, 12)), v_tiles=(12, 13, 14, 15),
        n_groups=2, n_eff=D_HEAD, shift=ROT_DIM // 2,
        latent=(g_cq[0], g_ckv[0], wq, wk, wv, _pad_lanes(gk_c[0]), cos_c, sin_c, dqc,
                C_ROPE // 2))
    oc = _attention(q_c, lambda p: (2 * p, 2 * p + 1), kt_c, lambda p: (2 * p, 2 * p + 1),
                    vp_c, lambda p: p, _pad_lanes(gq_c[0]), cos_c, sin_c, unit_coef, ones_row,
                    batch=batch, seq=seq, n_pairs=4, n_groups=1, n_eff=dqc,
                    shift=C_ROPE // 2, scale=dqc ** -0.5, split_q=False, combine="select",
                    post_norm=False)

    lam_init = 0.8 - 0.6 * math.exp(-0.3 * 1)
    lam = (jnp.exp(jnp.sum(lam_q1[0].astype(F32) * lam_k1[0].astype(F32)))
           - jnp.exp(jnp.sum(lam_q2[0].astype(F32) * lam_k2[0].astype(F32))) + lam_init)
    diff_coef = jnp.full((8, ATTN_TQ), -lam, F32)
    gpost = (g_sub_d[0].astype(F32) * (1.0 - lam_init)).reshape(1, LANES)
    od = _attention(proj_o, lambda p: (4 + p, 4 + p), kt_d, lambda p: (p, p),
                    vp_d, lambda p: p, _dup(gq_d[0]), cos_d, sin_d, diff_coef, gpost,
                    batch=batch, seq=seq, n_pairs=4, n_groups=2, n_eff=D_HEAD,
                    shift=ROT_DIM // 2, scale=D_HEAD ** -0.5, split_q=True, combine="add",
                    post_norm=True)
    x2 = _proj_out(oc, od, gate_o, w_out_o[0].astype(BF16), x2)
    return x2.reshape(batch, seq, d)
```

```python
import functools
import math

import jax
import jax.numpy as jnp
import numpy as np
from jax import lax
from jax.experimental import pallas as pl
from jax.experimental.pallas import tpu as pltpu

F32 = jnp.float32
BF16 = jnp.bfloat16

LANES = 128
SUBLANES = 8
VMEM_LIMIT_BYTES = 56 * 1024 * 1024

D_HEAD = 64
GRID_W = 64
RMS_EPS = 1e-6
WIN_R = 8
WIN_C = 16
AXIAL_THETA = 10000.0
MLA_THETA = 10000.0
ROPE_THETA = 500000.0
C_NOPE = 64
C_ROPE = 32
C_V = 64
ROT_DIM = 16
NEG_BIAS = -1e30

PROJ_TM = 512
PROJ_TN = 512
ATTN_TQ = 256
ATTN_STEP_PAIRS = 1
ATTN_TK = 1024
ATTN_TKQ = {"select": 512, "add": 1024}
FOLD_WAYS = 4
PREP_ROWS = 512
ONES_ROWS = 16
VT_HALF = D_HEAD + ONES_ROWS


def _params(*sem):
    return pltpu.CompilerParams(dimension_semantics=sem,
                                vmem_limit_bytes=VMEM_LIMIT_BYTES)


def _lane_iota(shape):
    return lax.broadcasted_iota(jnp.int32, shape, len(shape) - 1)


def _group_norm(x, gain, n_groups, n_eff):
    y = x * x
    tot = jnp.sum(y, axis=-1, keepdims=True)
    if n_groups == 1:
        ms = tot * (1.0 / n_eff)
    else:
        lo_mask = _lane_iota(x.shape) < D_HEAD
        lo = jnp.sum(jnp.where(lo_mask, y, 0.0), axis=-1, keepdims=True)
        ms = jnp.where(lo_mask, lo, tot - lo) * (1.0 / D_HEAD)
    return x * lax.rsqrt(ms + RMS_EPS) * gain


def _rotate(x, cos, sin_signed, shift):
    n = x.shape[-1]
    up = pltpu.roll(x, n - shift, 1)
    down = pltpu.roll(x, shift, 1)
    partner = jnp.where((_lane_iota(x.shape) & shift) == 0, up, down)
    return x * cos + partner * sin_signed


def _store_keys(kn_ref, j, tile, gain, cos, sin, *, n_groups, n_eff, shift, rotate=True):
    k = _group_norm(tile, gain, n_groups, n_eff)
    if rotate:
        k = _rotate(k, cos, sin, shift)
    kn_ref[0, j] = k.astype(BF16)


def _store_values_t(vt_ref, j, tile):
    vt = tile.T.astype(BF16)
    ones = jnp.ones((ONES_ROWS, vt.shape[1]), BF16)
    for h in range(2):
        base = h * VT_HALF
        vt_ref[0, j, base:base + D_HEAD, :] = vt[h * D_HEAD:(h + 1) * D_HEAD]
        vt_ref[0, j, base + D_HEAD:base + VT_HALF, :] = ones


def _latent_up(cq, ckv, kpe, gq_ref, gkv_ref, wq_ref, wk_ref, wv_ref, gk_ref, cos_ref, sin_ref,
               q_ref, kn_ref, vt_ref, *, n_eff, shift):
    def normed(x, g):
        ms = jnp.mean(x * x, axis=-1, keepdims=True)
        return (x * lax.rsqrt(ms + RMS_EPS) * g).astype(BF16)

    hq = normed(cq, gq_ref[...])
    hkv = normed(ckv, gkv_ref[...])
    q_ref[...] = jnp.dot(hq, wq_ref[...], preferred_element_type=F32)
    kk = jnp.dot(hkv, wk_ref[...], preferred_element_type=F32)
    gk = gk_ref[...]
    pe_rot = _rotate(kpe * gk, cos_ref[...], sin_ref[...], shift)
    ss_pe = jnp.sum(kpe * kpe, axis=-1, keepdims=True)
    for h in range(kn_ref.shape[1]):
        nope = kk[:, h * LANES:(h + 1) * LANES]
        ss = jnp.sum(nope * nope, axis=-1, keepdims=True) + ss_pe
        r = lax.rsqrt(ss * (1.0 / n_eff) + RMS_EPS)
        kn_ref[0, h] = ((nope * gk + pe_rot) * r).astype(BF16)
    vv = jnp.dot(hkv, wv_ref[...], preferred_element_type=F32)
    for j in range(vt_ref.shape[1]):
        _store_values_t(vt_ref, j, vv[:, j * LANES:(j + 1) * LANES])


N_LATENT_IN = 8


def _proj_in_kernel(*refs, k_tiles, v_tiles, n_groups, n_eff, shift, latent):
    n_in = 6 + (N_LATENT_IN if latent else 0)
    x_ref, g_ref, w_ref, gk_ref, cos_ref, sin_ref = refs[:6]
    o_ref, gate_ref, kn_ref, vt_ref = refs[n_in:n_in + 4]
    x = x_ref[...]
    ms = jnp.mean(x * x, axis=-1, keepdims=True)
    h = (x * lax.rsqrt(ms + RMS_EPS) * g_ref[...]).astype(BF16)
    n_mix = o_ref.shape[1]
    for n0 in range(0, w_ref.shape[1], PROJ_TN):
        y = jnp.dot(h, w_ref[:, n0:n0 + PROJ_TN], preferred_element_type=F32)
        if n0 >= n_mix:
            gate_ref[:, n0 - n_mix:n0 - n_mix + PROJ_TN] = y.astype(BF16)
            continue
        o_ref[:, n0:n0 + PROJ_TN] = y
        if latent and n0 == 0:
            ql, kvl, lat_eff, lat_shift = latent
            _latent_up(o_ref[:, :ql], o_ref[:, ql:ql + kvl], o_ref[:, ql + kvl:ql + kvl + LANES],
                       *refs[6:n_in], *refs[n_in + 4:], n_eff=lat_eff, shift=lat_shift)
        for j, (t, g, rotate) in enumerate(k_tiles):
            if n0 <= t * LANES < n0 + PROJ_TN:
                _store_keys(kn_ref, j, o_ref[:, t * LANES:(t + 1) * LANES],
                            gk_ref[g * SUBLANES:g * SUBLANES + 1, :],
                            cos_ref[...], sin_ref[...], n_groups=n_groups, n_eff=n_eff,
                            shift=shift, rotate=rotate)
        for j, t in enumerate(v_tiles):
            if n0 <= t * LANES < n0 + PROJ_TN:
                _store_values_t(vt_ref, j, o_ref[:, t * LANES:(t + 1) * LANES])


def _proj_in(x2, gain, w, gk, cos, sin, *, batch, seq, n_gate, k_tiles, v_tiles, n_groups,
             n_eff, shift, latent=None):
    m, d = x2.shape
    n = w.shape[1]
    n_mix = n - n_gate
    tm = PROJ_TM
    nblk = seq // tm
    nk, nv = len(k_tiles), len(v_tiles)
    const = lambda i: (0, 0)
    rows = lambda i: (i % nblk, 0)
    tiles = lambda i: (i // nblk, 0, i % nblk, 0)
    tiles_t = lambda i: (i // nblk, 0, 0, i % nblk)
    operands = [x2, gain.reshape(1, d), w, jnp.repeat(gk, SUBLANES, axis=0), cos, sin]
    in_specs = [pl.BlockSpec((tm, d), lambda i: (i, 0)),
                pl.BlockSpec((1, d), const),
                pl.BlockSpec((d, n), const),
                pl.BlockSpec((gk.shape[0] * SUBLANES, LANES), const),
                pl.BlockSpec((tm, LANES), rows),
                pl.BlockSpec((tm, LANES), rows)]
    out_specs = [pl.BlockSpec((tm, n_mix), lambda i: (i, 0)),
                 pl.BlockSpec((tm, n_gate), lambda i: (i, 0)),
                 pl.BlockSpec((1, nk, tm, LANES), tiles),
                 pl.BlockSpec((1, nv, 2 * VT_HALF, tm), tiles_t)]
    out_shape = [jax.ShapeDtypeStruct((m, n_mix), F32),
                 jax.ShapeDtypeStruct((m, n_gate), BF16),
                 jax.ShapeDtypeStruct((batch, nk, seq, LANES), BF16),
                 jax.ShapeDtypeStruct((batch, nv, 2 * VT_HALF, seq), BF16)]
    lat_static = None
    if latent is not None:
        g_cq, g_ckv, wq, wk, wv, gk_l, cos_l, sin_l, lat_eff, lat_shift = latent
        ql, kvl = wq.shape[0], wk.shape[0]
        assert ql + kvl + LANES <= PROJ_TN
        nkl, nvl = wk.shape[1] // LANES, wv.shape[1] // LANES
        lat_static = (ql, kvl, lat_eff, lat_shift)
        operands += [g_cq.reshape(1, ql), g_ckv.reshape(1, kvl), wq, wk, wv, gk_l, cos_l, sin_l]
        in_specs += [pl.BlockSpec((1, ql), const), pl.BlockSpec((1, kvl), const),
                     pl.BlockSpec(wq.shape, const), pl.BlockSpec(wk.shape, const),
                     pl.BlockSpec(wv.shape, const), pl.BlockSpec((1, LANES), const),
                     pl.BlockSpec((tm, LANES), rows), pl.BlockSpec((tm, LANES), rows)]
        assert len(operands) == 6 + N_LATENT_IN
        out_specs += [pl.BlockSpec((tm, wq.shape[1]), lambda i: (i, 0)),
                      pl.BlockSpec((1, nkl, tm, LANES), tiles),
                      pl.BlockSpec((1, nvl, 2 * VT_HALF, tm), tiles_t)]
        out_shape += [jax.ShapeDtypeStruct((m, wq.shape[1]), F32),
                      jax.ShapeDtypeStruct((batch, nkl, seq, LANES), BF16),
                      jax.ShapeDtypeStruct((batch, nvl, 2 * VT_HALF, seq), BF16)]
    kern = functools.partial(_proj_in_kernel, k_tiles=k_tiles, v_tiles=v_tiles,
                             n_groups=n_groups, n_eff=n_eff, shift=shift, latent=lat_static)
    return pl.pallas_call(
        kern,
        grid=(m // tm,),
        in_specs=in_specs,
        out_specs=out_specs,
        out_shape=out_shape,
        compiler_params=_params("parallel"),
        name="proj_in",
    )(*operands)


def _proj_out_kernel(oa_ref, ob_ref, ga_ref, gb_ref, w_ref, x_ref, o_ref):
    def gated(o_r, g_r):
        g = g_r[...].astype(F32)
        return (o_r[...].astype(F32) * (g / (1.0 + jnp.exp(-g)))).astype(BF16)

    half = oa_ref.shape[1]
    y = jnp.dot(gated(oa_ref, ga_ref), w_ref[:half, :], preferred_element_type=F32)
    y = y + jnp.dot(gated(ob_ref, gb_ref), w_ref[half:, :], preferred_element_type=F32)
    o_ref[...] = x_ref[...] + y


def _proj_out(oa, ob, gate, w, x2):
    m, d = x2.shape
    half = oa.shape[1]
    return pl.pallas_call(
        _proj_out_kernel,
        grid=(m // PROJ_TM,),
        in_specs=[pl.BlockSpec((PROJ_TM, half), lambda i: (i, 0)),
                  pl.BlockSpec((PROJ_TM, half), lambda i: (i, 0)),
                  pl.BlockSpec((PROJ_TM, half), lambda i: (i, 0)),
                  pl.BlockSpec((PROJ_TM, half), lambda i: (i, 1)),
                  pl.BlockSpec((2 * half, d), lambda i: (0, 0)),
                  pl.BlockSpec((PROJ_TM, d), lambda i: (i, 0))],
        out_specs=pl.BlockSpec((PROJ_TM, d), lambda i: (i, 0)),
        out_shape=jax.ShapeDtypeStruct((m, d), F32),
        compiler_params=_params("parallel"),
        name="proj_out",
    )(oa, ob, gate, gate, w, x2)


def _attn_kernel(*refs, n_step_pairs, n_groups, n_eff, shift, qscale, split_q, combine,
                 post_norm):
    g2 = 2 * n_step_pairs
    q_refs = [refs[2 * pp:2 * pp + 2] for pp in range(n_step_pairs)]
    gq_ref, cos_ref, sin_ref = refs[g2:g2 + 3]
    k_refs = [refs[g2 + 3 + 2 * pp:g2 + 5 + 2 * pp] for pp in range(n_step_pairs)]
    vt_refs = refs[2 * g2 + 3:2 * g2 + 3 + n_step_pairs]
    (coef_ref, gpost_ref, o_ref, s0a_scr, s1a_scr, s0b_scr, s1b_scr, qt_scr,
     res_scr) = refs[2 * g2 + 3 + n_step_pairs:]
    seq = vt_refs[0].shape[3]
    tq = ATTN_TQ
    nq = seq // tq
    s_scrs = ((s0a_scr, s1a_scr), (s0b_scr, s1b_scr))
    lane = _lane_iota((tq, LANES))

    def block_rows(i):
        if isinstance(i, int):
            return pl.ds(i * tq, tq)
        return pl.ds(pl.multiple_of(i * tq, tq), tq)

    def fold8(x, op):
        part = op(x.reshape(FOLD_WAYS, x.shape[0] // (8 * FOLD_WAYS), 8, x.shape[1]), axis=1)
        return op(part, axis=0)

    def tree(parts, op):
        while len(parts) > 1:
            parts = [op(a, b) for a, b in zip(parts[::2], parts[1::2])] + parts[len(parts) & ~1:]
        return parts[0]

    def prep_q(pp, i, u, par):
        rows = block_rows(i)
        q = _group_norm(q_refs[pp][u][rows, :], gq_ref[...], n_groups, n_eff)
        q = _rotate(q, cos_ref[rows, :], sin_ref[rows, :], shift) * qscale
        if split_q:
            q = jnp.where((lane < D_HEAD) == (u == 0), q, 0.0)
        qt_scr[2 * par + u] = q.T.astype(BF16)

    def tick(scores=None, softmax=None):
        acc = None
        maxes = []
        if scores is not None:
            sp, su, spar = scores
            qt = qt_scr[2 * spar + su]
        for c0 in range(0, seq, ATTN_TK):
            ks = slice(c0, c0 + ATTN_TK)
            if scores is not None:
                for k0 in range(c0, c0 + ATTN_TK, ATTN_TKQ[combine]):
                    kq = slice(k0, k0 + ATTN_TKQ[combine])
                    s = jnp.dot(k_refs[sp][su][0, 0, kq, :], qt, preferred_element_type=F32)
                    s_scrs[spar][su][kq, :] = s
                    maxes.append(fold8(s, jnp.max))
            if softmax is not None:
                pp, u, par, m = softmax
                p = jnp.exp2(s_scrs[par][u][ks, :] - m).astype(BF16)
                vrows = (slice(u * VT_HALF, (u + 1) * VT_HALF)
                         if combine == "select" else slice(0, VT_HALF + D_HEAD))
                pv = jnp.dot(vt_refs[pp][0, 0, vrows, ks], p, preferred_element_type=F32)
                acc = pv if acc is None else acc + pv
        m_new = jnp.max(tree(maxes, jnp.maximum), axis=0, keepdims=True) if maxes else None
        res = None
        if acc is not None:
            inv_l = 1.0 / acc[D_HEAD:D_HEAD + 1]
            halves = [acc[h:h + D_HEAD] for h in range(0, acc.shape[0], VT_HALF)]
            res = jnp.concatenate(halves, axis=0) * inv_l
        return m_new, res

    def emit(pp, i, res_e, res_o):
        if combine == "select":
            out_t = jnp.concatenate([res_e, res_o], axis=0)
        else:
            out_t = res_e + coef_ref[0:1, :] * res_o
        if post_norm:
            ms = jnp.mean(out_t * out_t, axis=0, keepdims=True)
            out_t = out_t * lax.rsqrt(ms + RMS_EPS)
        o_ref[block_rows(i), pp * LANES:(pp + 1) * LANES] = (
            out_t.T * gpost_ref[...]).astype(o_ref.dtype)

    assert n_step_pairs == 1 and nq % 2 == 0
    pp = 0

    def advance(i, par, m_o):
        nxt = 1 - par
        m_e_next, res_o = tick(scores=(pp, 0, nxt), softmax=(pp, 1, par, m_o))
        ahead = min(i + 2, nq - 1) if isinstance(i, int) else jnp.minimum(i + 2, nq - 1)
        prep_q(pp, ahead, 0, par)
        emit(pp, i, res_scr[par], res_o)
        m_o_next, res_e = tick(scores=(pp, 1, nxt), softmax=(pp, 0, nxt, m_e_next))
        prep_q(pp, ahead, 1, par)
        res_scr[nxt] = res_e
        return m_o_next

    prep_q(pp, 0, 0, 0)
    prep_q(pp, 0, 1, 0)
    m_e, _ = tick(scores=(pp, 0, 0))
    prep_q(pp, 1, 0, 1)
    m_o, res_e = tick(scores=(pp, 1, 0), softmax=(pp, 0, 0, m_e))
    res_scr[0] = res_e
    prep_q(pp, 1, 1, 1)

    def body(k, m_o):
        return advance(2 * k + 1, 1, advance(2 * k, 0, m_o))

    m_o = lax.fori_loop(0, (nq - 1) // 2, body, m_o)
    m_o = advance(nq - 2, 0, m_o)
    _, res_o = tick(softmax=(pp, 1, 1, m_o))
    emit(pp, nq - 1, res_scr[1], res_o)


def _attention(q_src, q_blk, kn, k_idx, vt, v_idx, gq, cos, sin, coef, gpost, *,
               batch, seq, n_pairs, n_groups, n_eff, shift, scale, split_q, combine,
               post_norm):
    tq = ATTN_TQ
    m = batch * seq
    g = ATTN_STEP_PAIRS
    const = lambda b, s: (0, 0)

    q_specs = [pl.BlockSpec((seq, LANES), lambda b, s, pp=pp, w=w: (b, q_blk(s * g + pp)[w]))
               for pp in range(g) for w in range(2)]
    k_specs = [pl.BlockSpec((1, 1, seq, LANES),
                            lambda b, s, pp=pp, w=w: (b, k_idx(s * g + pp)[w], 0, 0))
               for pp in range(g) for w in range(2)]
    v_specs = [pl.BlockSpec((1, 1, 2 * VT_HALF, seq),
                            lambda b, s, pp=pp: (b, v_idx(s * g + pp), 0, 0))
               for pp in range(g)]
    table = pl.BlockSpec((seq, LANES), const, pipeline_mode=pl.Buffered(1))
    kern = functools.partial(_attn_kernel, n_step_pairs=g, n_groups=n_groups, n_eff=n_eff,
                             shift=shift, qscale=scale * math.log2(math.e), split_q=split_q,
                             combine=combine, post_norm=post_norm)
    return pl.pallas_call(
        kern,
        grid=(batch, n_pairs // g),
        in_specs=(q_specs + [pl.BlockSpec((1, LANES), const), table, table] + k_specs
                  + v_specs + [pl.BlockSpec((8, tq), const), pl.BlockSpec((1, LANES), const)]),
        out_specs=pl.BlockSpec((seq, g * LANES), lambda b, s: (b, s)),
        out_shape=jax.ShapeDtypeStruct((m, n_pairs * LANES), BF16),
        scratch_shapes=[pltpu.VMEM((seq, tq), F32)] * 4 + [
            pltpu.VMEM((4, LANES, tq), BF16),
            pltpu.VMEM((2, D_HEAD if combine == "select" else LANES, tq), F32)],
        compiler_params=_params("parallel", "parallel"),
        name="attention",
    )(*([q_src] * (2 * g)), gq, cos, sin, *([kn] * (2 * g)), *([vt] * g), coef, gpost)


def _natten_kernel(q_ref, k_ref, v_ref, bias_ref, o_ref, q_scr, v_scr, *, rows):
    seq = v_ref.shape[0]
    nwin = WIN_R * GRID_W
    for r0 in range(0, seq, PREP_ROWS):
        sl = slice(r0, r0 + PREP_ROWS)
        q = q_ref[0, 0, sl, :]
        lo = _lane_iota(q.shape) < D_HEAD
        zero = jnp.zeros_like(q)
        q_scr[0, sl, :] = jnp.where(lo, q, zero)
        q_scr[1, sl, :] = jnp.where(lo, zero, q)
        v_scr[sl, :LANES] = v_ref[sl, :].astype(BF16)
        v_scr[sl, LANES:] = jnp.ones((PREP_ROWS, LANES), BF16)

    lane_lo = _lane_iota((GRID_W, LANES)) < D_HEAD

    def body(r, carry):
        w0 = jnp.clip(r - WIN_R // 2, 0, rows - WIN_R)
        var = r - w0
        kbase = pl.multiple_of(w0 * GRID_W, GRID_W)
        qbase = pl.multiple_of(r * GRID_W, GRID_W)
        kw = k_ref[0, 0, pl.ds(kbase, nwin), :]
        vw = v_scr[pl.ds(kbase, nwin), :]
        qrows = pl.ds(qbase, GRID_W)
        q2 = jnp.concatenate([q_scr[0, qrows, :], q_scr[1, qrows, :]], axis=0)
        s = lax.dot_general(q2, kw, (((1,), (1,)), ((), ())),
                            preferred_element_type=F32)
        z = s + bias_ref[0, var]
        zmax = jnp.max(z, axis=-1, keepdims=True)
        p = jnp.exp2(z - zmax).astype(BF16)
        acc = jnp.dot(p, vw, preferred_element_type=F32)
        res = acc[:, :LANES] / acc[:, LANES:]
        o_ref[qrows, :] = jnp.where(lane_lo, res[:GRID_W], res[GRID_W:]).astype(o_ref.dtype)
        return carry

    lax.fori_loop(0, rows, body, 0, unroll=16)


def _natten(tiles, proj, v_blk0, bias, batch, seq):
    rows = seq // GRID_W
    n_pairs = bias.shape[0]
    nwin = WIN_R * GRID_W
    return pl.pallas_call(
        functools.partial(_natten_kernel, rows=rows),
        grid=(batch, n_pairs),
        in_specs=[pl.BlockSpec((1, 1, seq, LANES), lambda b, p: (b, p, 0, 0)),
                  pl.BlockSpec((1, 1, seq, LANES), lambda b, p: (b, n_pairs + p, 0, 0)),
                  pl.BlockSpec((seq, LANES), lambda b, p: (b, v_blk0 + p)),
                  pl.BlockSpec((1, WIN_R, 2 * GRID_W, nwin), lambda b, p: (p, 0, 0, 0))],
        out_specs=pl.BlockSpec((seq, LANES), lambda b, p: (b, p)),
        out_shape=jax.ShapeDtypeStruct((batch * seq, n_pairs * LANES), BF16),
        scratch_shapes=[pltpu.VMEM((2, seq, LANES), BF16),
                        pltpu.VMEM((seq, 2 * LANES), BF16)],
        compiler_params=_params("parallel", "parallel"),
        name="natten",
    )(tiles, tiles, proj, bias)


def _natten_bias(rpb):
    col = np.arange(GRID_W)
    c0 = np.clip(col - WIN_C // 2, 0, GRID_W - WIN_C)
    ci = np.arange(GRID_W)
    inside = (ci[None, :] >= c0[:, None]) & (ci[None, :] < c0[:, None] + WIN_C)
    cb = ci[None, :] - col[:, None] + (WIN_C - 1)
    onehot = (cb[:, :, None] == np.arange(2 * WIN_C - 1)) & inside[:, :, None]
    toep = jnp.einsum("hrb,cjb->hrcj", rpb.astype(F32), jnp.asarray(onehot, F32),
                      precision=lax.Precision.HIGHEST)
    toep = jnp.where(inside[None, None], toep * math.log2(math.e), NEG_BIAS)
    n_heads = rpb.shape[0]
    variants = []
    for v in range(WIN_R):
        rows = toep[:, WIN_R - 1 - v:2 * WIN_R - 1 - v]
        variants.append(rows.transpose(0, 2, 1, 3).reshape(n_heads // 2, 2 * GRID_W,
                                                           WIN_R * GRID_W))
    return jnp.stack(variants, axis=1)


def _rot_tables(parts):
    ang = 0.0
    sign, off = [], 0
    for part in parts:
        if isinstance(part, int):
            sign.append(np.zeros(part, np.float32))
            off += part
            continue
        pos, dim, theta = part
        inv = jnp.power(theta, -jnp.arange(0, dim, 2, dtype=F32) / dim)
        inv_lanes = jnp.pad(jnp.concatenate([inv, inv]), (off, LANES - off - dim))
        ang = ang + pos[:, None] * inv_lanes[None, :]
        sign.append(np.repeat(np.float32([-1.0, 1.0]), dim // 2))
        off += dim
    return jnp.cos(ang), jnp.sin(ang) * jnp.asarray(np.concatenate(sign))[None, :]


def _dup(v):
    return jnp.concatenate([v, v]).reshape(1, -1).astype(F32)


def _pad_lanes(v):
    return jnp.pad(v.astype(F32), (0, LANES - v.shape[0])).reshape(1, LANES)


def kernel(x, norm_e, w_in_e, gq_a, gk_a, rpb_a, gq_b, gk_b, w_out_e, norm_o, w_in_o, g_cq, w_cq_b, g_ckv, w_ckv_b, gq_c, gk_c, gq_d, gk_d, lam_q1, lam_k1, lam_q2, lam_k2, g_sub_d, w_out_o):
    batch, seq, d = x.shape
    m = batch * seq
    x2 = x.reshape(m, d)
    t = jnp.arange(seq)
    pos = t.astype(F32)
    row = (t // GRID_W).astype(F32)
    col = (t % GRID_W).astype(F32)
    half = D_HEAD // 2
    ones_row = jnp.ones((1, LANES), F32)
    unit_coef = jnp.ones((8, ATTN_TQ), F32)

    n_a = 8 * D_HEAD
    base_b = 3 * n_a
    kb0 = base_b + 8 * D_HEAD
    vb0 = kb0 + 2 * D_HEAD
    gate0 = vb0 + 2 * D_HEAD
    w0 = w_in_e[0]
    dup_heads = [w0[:, c:c + D_HEAD] for c in (kb0, kb0, kb0 + D_HEAD, kb0 + D_HEAD,
                                                 vb0, vb0, vb0 + D_HEAD, vb0 + D_HEAD)]
    w_e = jnp.concatenate([w0[:, :kb0]] + dup_heads + [w0[:, gate0:]],
                          axis=1).astype(BF16)
    axial = [(row, half, AXIAL_THETA), (col, half, AXIAL_THETA)]
    cos_b, sin_b = _rot_tables(axial + axial)
    gains_e = jnp.concatenate([_dup(gq_a[0]) * (D_HEAD ** -0.5 * math.log2(math.e)),
                               _dup(gk_a[0]), _dup(gk_b[0])], axis=0)
    tiles_e = (tuple((t, 0, False) for t in range(0, 4))
               + tuple((t, 1, False) for t in range(4, 8)) + ((16, 2, True), (17, 2, True)))
    proj_e, gate_e, kt_e, vp_b = _proj_in(
        x2, norm_e[0], w_e, gains_e, cos_b, sin_b, batch=batch, seq=seq, n_gate=d,
        k_tiles=tiles_e, v_tiles=(18, 19), n_groups=2, n_eff=D_HEAD, shift=half // 2)
    bias = _natten_bias(rpb_a[0])
    oa = _natten(kt_e, proj_e, 8, bias, batch, seq)
    ob = _attention(proj_e, lambda p: (12 + p, 12 + p), kt_e,
                    lambda p: (8 + p // 2, 8 + p // 2),
                    vp_b, lambda p: p // 2, _dup(gq_b[0]), cos_b, sin_b, unit_coef, ones_row,
                    batch=batch, seq=seq, n_pairs=4, n_groups=2, n_eff=D_HEAD,
                    shift=half // 2, scale=D_HEAD ** -0.5, split_q=True, combine="select",
                    post_norm=False)
    x2 = _proj_out(oa, ob, gate_e, w_out_e[0].astype(BF16), x2)

    w1 = w_in_o[0]
    q_lora, kv_lora = g_cq.shape[1], g_ckv.shape[1]
    o_kpe = q_lora + kv_lora
    o_qd = o_kpe + C_ROPE
    zeros = functools.partial(jnp.zeros, dtype=w1.dtype)
    w_o = jnp.concatenate([
        w1[:, :o_kpe],
        zeros((d, C_NOPE)), w1[:, o_kpe:o_qd], zeros((d, LANES - C_NOPE - C_ROPE)),
        w1[:, o_qd:]], axis=1).astype(BF16)
    partial = [(pos, ROT_DIM, ROPE_THETA), D_HEAD - ROT_DIM]
    cos_d, sin_d = _rot_tables(partial + partial)
    n_hc = 8
    dqc = C_NOPE + C_ROPE
    wq = jnp.pad(w_cq_b[0].reshape(q_lora, n_hc, dqc),
                 ((0, 0), (0, 0), (0, LANES - dqc))).reshape(q_lora, n_hc * LANES).astype(BF16)
    wkv = w_ckv_b[0].reshape(kv_lora, n_hc, C_NOPE + C_V)
    wk = jnp.pad(wkv[:, :, :C_NOPE],
                 ((0, 0), (0, 0), (0, LANES - C_NOPE))).reshape(kv_lora, n_hc * LANES).astype(BF16)
    wv = wkv[:, :, C_NOPE:].reshape(kv_lora, n_hc * C_V).astype(BF16)
    cos_c, sin_c = _rot_tables([C_NOPE, (pos, C_ROPE, MLA_THETA), LANES - C_NOPE - C_ROPE])
    proj_o, gate_o, kt_d, vp_d, q_c, kt_c, vp_c = _proj_in(
        x2, norm_o[0], w_o, _dup(gk_d[0]), cos_d, sin_d, batch=batch, seq=seq, n_gate=d,
        k_tiles=tuple((t, 0, True) for t in range(8, 12)), v_tiles=(12, 13, 14, 15),
        n_groups=2, n_eff=D_HEAD, shift=ROT_DIM // 2,
        latent=(g_cq[0], g_ckv[0], wq, wk, wv, _pad_lanes(gk_c[0]), cos_c, sin_c, dqc,
                C_ROPE // 2))
    oc = _attention(q_c, lambda p: (2 * p, 2 * p + 1), kt_c, lambda p: (2 * p, 2 * p + 1),
                    vp_c, lambda p: p, _pad_lanes(gq_c[0]), cos_c, sin_c, unit_coef, ones_row,
                    batch=batch, seq=seq, n_pairs=4, n_groups=1, n_eff=dqc,
                    shift=C_ROPE // 2, scale=dqc ** -0.5, split_q=False, combine="select",
                    post_norm=False)

    lam_init = 0.8 - 0.6 * math.exp(-0.3 * 1)
    lam = (jnp.exp(jnp.sum(lam_q1[0].astype(F32) * lam_k1[0].astype(F32)))
           - jnp.exp(jnp.sum(lam_q2[0].astype(F32) * lam_k2[0].astype(F32))) + lam_init)
    diff_coef = jnp.full((8, ATTN_TQ), -lam, F32)
    gpost = (g_sub_d[0].astype(F32) * (1.0 - lam_init)).reshape(1, LANES)
    od = _attention(proj_o, lambda p: (4 + p, 4 + p), kt_d, lambda p: (p, p),
                    vp_d, lambda p: p, _dup(gq_d[0]), cos_d, sin_d, diff_coef, gpost,
                    batch=batch, seq=seq, n_pairs=4, n_groups=2, n_eff=D_HEAD,
                    shift=ROT_DIM // 2, scale=D_HEAD ** -0.5, split_q=True, combine="add",
                    post_norm=True)
    x2 = _proj_out(oc, od, gate_o, w_out_o[0].astype(BF16), x2)
    return x2.reshape(batch, seq, d)
```

```python
import functools
import math

import jax
import jax.numpy as jnp
import numpy as np
from jax import lax
from jax.experimental import pallas as pl
from jax.experimental.pallas import tpu as pltpu

F32 = jnp.float32
BF16 = jnp.bfloat16

LANES = 128
SUBLANES = 8
VMEM_LIMIT_BYTES = 56 * 1024 * 1024

D_HEAD = 64
GRID_W = 64
RMS_EPS = 1e-6
WIN_R = 8
WIN_C = 16
AXIAL_THETA = 10000.0
MLA_THETA = 10000.0
ROPE_THETA = 500000.0
C_NOPE = 64
C_ROPE = 32
C_V = 64
ROT_DIM = 16
NEG_BIAS = -1e30

PROJ_TM = 512
PROJ_TN = 512
ATTN_TQ = 256
ATTN_STEP_PAIRS = 1
ATTN_TK = 1024
ATTN_TKQ = {"select": 512, "add": 1024}
FOLD_WAYS = 4
PREP_ROWS = 512
ONES_ROWS = 16
VT_HALF = D_HEAD + ONES_ROWS


def _params(*sem):
    return pltpu.CompilerParams(dimension_semantics=sem,
                                vmem_limit_bytes=VMEM_LIMIT_BYTES)


def _lane_iota(shape):
    return lax.broadcasted_iota(jnp.int32, shape, len(shape) - 1)


def _group_norm(x, gain, n_groups, n_eff):
    y = x * x
    tot = jnp.sum(y, axis=-1, keepdims=True)
    if n_groups == 1:
        ms = tot * (1.0 / n_eff)
    else:
        lo_mask = _lane_iota(x.shape) < D_HEAD
        lo = jnp.sum(jnp.where(lo_mask, y, 0.0), axis=-1, keepdims=True)
        ms = jnp.where(lo_mask, lo, tot - lo) * (1.0 / D_HEAD)
    return x * lax.rsqrt(ms + RMS_EPS) * gain


def _rotate(x, cos, sin_signed, shift):
    n = x.shape[-1]
    up = pltpu.roll(x, n - shift, 1)
    down = pltpu.roll(x, shift, 1)
    partner = jnp.where((_lane_iota(x.shape) & shift) == 0, up, down)
    return x * cos + partner * sin_signed


def _store_keys(kn_ref, j, tile, gain, cos, sin, *, n_groups, n_eff, shift, rotate=True):
    k = _group_norm(tile, gain, n_groups, n_eff)
    if rotate:
        k = _rotate(k, cos, sin, shift)
    kn_ref[0, j] = k.astype(BF16)


def _store_values_t(vt_ref, j, tile):
    vt = tile.T.astype(BF16)
    ones = jnp.ones((ONES_ROWS, vt.shape[1]), BF16)
    for h in range(2):
        base = h * VT_HALF
        vt_ref[0, j, base:base + D_HEAD, :] = vt[h * D_HEAD:(h + 1) * D_HEAD]
        vt_ref[0, j, base + D_HEAD:base + VT_HALF, :] = ones


def _latent_up(cq, ckv, kpe, gq_ref, gkv_ref, wq_ref, wk_ref, wv_ref, gk_ref, cos_ref, sin_ref,
               q_ref, kn_ref, vt_ref, *, n_eff, shift):
    def normed(x, g):
        ms = jnp.mean(x * x, axis=-1, keepdims=True)
        return (x * lax.rsqrt(ms + RMS_EPS) * g).astype(BF16)

    hq = normed(cq, gq_ref[...])
    hkv = normed(ckv, gkv_ref[...])
    q_ref[...] = jnp.dot(hq, wq_ref[...], preferred_element_type=F32)
    kk = jnp.dot(hkv, wk_ref[...], preferred_element_type=F32)
    gk = gk_ref[...]
    pe_rot = _rotate(kpe * gk, cos_ref[...], sin_ref[...], shift)
    ss_pe = jnp.sum(kpe * kpe, axis=-1, keepdims=True)
    for h in range(kn_ref.shape[1]):
        nope = kk[:, h * LANES:(h + 1) * LANES]
        ss = jnp.sum(nope * nope, axis=-1, keepdims=True) + ss_pe
        r = lax.rsqrt(ss * (1.0 / n_eff) + RMS_EPS)
        kn_ref[0, h] = ((nope * gk + pe_rot) * r).astype(BF16)
    vv = jnp.dot(hkv, wv_ref[...], preferred_element_type=F32)
    for j in range(vt_ref.shape[1]):
        _store_values_t(vt_ref, j, vv[:, j * LANES:(j + 1) * LANES])


N_LATENT_IN = 8


def _proj_in_kernel(*refs, k_tiles, v_tiles, n_groups, n_eff, shift, latent):
    n_in = 6 + (N_LATENT_IN if latent else 0)
    x_ref, g_ref, w_ref, gk_ref, cos_ref, sin_ref = refs[:6]
    o_ref, gate_ref, kn_ref, vt_ref = refs[n_in:n_in + 4]
    x = x_ref[...]
    ms = jnp.mean(x * x, axis=-1, keepdims=True)
    h = (x * lax.rsqrt(ms + RMS_EPS) * g_ref[...]).astype(BF16)
    n_mix = o_ref.shape[1]
    for n0 in range(0, w_ref.shape[1], PROJ_TN):
        y = jnp.dot(h, w_ref[:, n0:n0 + PROJ_TN], preferred_element_type=F32)
        if n0 >= n_mix:
            gate_ref[:, n0 - n_mix:n0 - n_mix + PROJ_TN] = y.astype(BF16)
            continue
        o_ref[:, n0:n0 + PROJ_TN] = y
        if latent and n0 == 0:
            ql, kvl, lat_eff, lat_shift = latent
            _latent_up(o_ref[:, :ql], o_ref[:, ql:ql + kvl], o_ref[:, ql + kvl:ql + kvl + LANES],
                       *refs[6:n_in], *refs[n_in + 4:], n_eff=lat_eff, shift=lat_shift)
        for j, (t, g, rotate) in enumerate(k_tiles):
            if n0 <= t * LANES < n0 + PROJ_TN:
                _store_keys(kn_ref, j, o_ref[:, t * LANES:(t + 1) * LANES],
                            gk_ref[g * SUBLANES:g * SUBLANES + 1, :],
                            cos_ref[...], sin_ref[...], n_groups=n_groups, n_eff=n_eff,
                            shift=shift, rotate=rotate)
        for j, t in enumerate(v_tiles):
            if n0 <= t * LANES < n0 + PROJ_TN:
                _store_values_t(vt_ref, j, o_ref[:, t * LANES:(t + 1) * LANES])


def _proj_in(x2, gain, w, gk, cos, sin, *, batch, seq, n_gate, k_tiles, v_tiles, n_groups,
             n_eff, shift, latent=None):
    m, d = x2.shape
    n = w.shape[1]
    n_mix = n - n_gate
    tm = PROJ_TM
    nblk = seq // tm
    nk, nv = len(k_tiles), len(v_tiles)
    const = lambda i: (0, 0)
    rows = lambda i: (i % nblk, 0)
    tiles = lambda i: (i // nblk, 0, i % nblk, 0)
    tiles_t = lambda i: (i // nblk, 0, 0, i % nblk)
    operands = [x2, gain.reshape(1, d), w, jnp.repeat(gk, SUBLANES, axis=0), cos, sin]
    in_specs = [pl.BlockSpec((tm, d), lambda i: (i, 0)),
                pl.BlockSpec((1, d), const),
                pl.BlockSpec((d, n), const),
                pl.BlockSpec((gk.shape[0] * SUBLANES, LANES), const),
                pl.BlockSpec((tm, LANES), rows),
                pl.BlockSpec((tm, LANES), rows)]
    out_specs = [pl.BlockSpec((tm, n_mix), lambda i: (i, 0)),
                 pl.BlockSpec((tm, n_gate), lambda i: (i, 0)),
                 pl.BlockSpec((1, nk, tm, LANES), tiles),
                 pl.BlockSpec((1, nv, 2 * VT_HALF, tm), tiles_t)]
    out_shape = [jax.ShapeDtypeStruct((m, n_mix), F32),
                 jax.ShapeDtypeStruct((m, n_gate), BF16),
                 jax.ShapeDtypeStruct((batch, nk, seq, LANES), BF16),
                 jax.ShapeDtypeStruct((batch, nv, 2 * VT_HALF, seq), BF16)]
    lat_static = None
    if latent is not None:
        g_cq, g_ckv, wq, wk, wv, gk_l, cos_l, sin_l, lat_eff, lat_shift = latent
        ql, kvl = wq.shape[0], wk.shape[0]
        assert ql + kvl + LANES <= PROJ_TN
        nkl, nvl = wk.shape[1] // LANES, wv.shape[1] // LANES
        lat_static = (ql, kvl, lat_eff, lat_shift)
        operands += [g_cq.reshape(1, ql), g_ckv.reshape(1, kvl), wq, wk, wv, gk_l, cos_l, sin_l]
        in_specs += [pl.BlockSpec((1, ql), const), pl.BlockSpec((1, kvl), const),
                     pl.BlockSpec(wq.shape, const), pl.BlockSpec(wk.shape, const),
                     pl.BlockSpec(wv.shape, const), pl.BlockSpec((1, LANES), const),
                     pl.BlockSpec((tm, LANES), rows), pl.BlockSpec((tm, LANES), rows)]
        assert len(operands) == 6 + N_LATENT_IN
        out_specs += [pl.BlockSpec((tm, wq.shape[1]), lambda i: (i, 0)),
                      pl.BlockSpec((1, nkl, tm, LANES), tiles),
                      pl.BlockSpec((1, nvl, 2 * VT_HALF, tm), tiles_t)]
        out_shape += [jax.ShapeDtypeStruct((m, wq.shape[1]), F32),
                      jax.ShapeDtypeStruct((batch, nkl, seq, LANES), BF16),
                      jax.ShapeDtypeStruct((batch, nvl, 2 * VT_HALF, seq), BF16)]
    kern = functools.partial(_proj_in_kernel, k_tiles=k_tiles, v_tiles=v_tiles,
                             n_groups=n_groups, n_eff=n_eff, shift=shift, latent=lat_static)
    return pl.pallas_call(
        kern,
        grid=(m // tm,),
        in_specs=in_specs,
        out_specs=out_specs,
        out_shape=out_shape,
        compiler_params=_params("parallel"),
        name="proj_in",
    )(*operands)


def _proj_out_kernel(oa_ref, ob_ref, ga_ref, gb_ref, w_ref, x_ref, o_ref):
    def gated(o_r, g_r):
        g = g_r[...].astype(F32)
        return (o_r[...].astype(F32) * (g / (1.0 + jnp.exp(-g)))).astype(BF16)

    half = oa_ref.shape[1]
    y = jnp.dot(gated(oa_ref, ga_ref), w_ref[:half, :], preferred_element_type=F32)
    y = y + jnp.dot(gated(ob_ref, gb_ref), w_ref[half:, :], preferred_element_type=F32)
    o_ref[...] = x_ref[...] + y


def _proj_out(oa, ob, gate, w, x2):
    m, d = x2.shape
    half = oa.shape[1]
    return pl.pallas_call(
        _proj_out_kernel,
        grid=(m // PROJ_TM,),
        in_specs=[pl.BlockSpec((PROJ_TM, half), lambda i: (i, 0)),
                  pl.BlockSpec((PROJ_TM, half), lambda i: (i, 0)),
                  pl.BlockSpec((PROJ_TM, half), lambda i: (i, 0)),
                  pl.BlockSpec((PROJ_TM, half), lambda i: (i, 1)),
                  pl.BlockSpec((2 * half, d), lambda i: (0, 0)),
                  pl.BlockSpec((PROJ_TM, d), lambda i: (i, 0))],
        out_specs=pl.BlockSpec((PROJ_TM, d), lambda i: (i, 0)),
        out_shape=jax.ShapeDtypeStruct((m, d), F32),
        compiler_params=_params("parallel"),
        name="proj_out",
    )(oa, ob, gate, gate, w, x2)


def _attn_kernel(*refs, n_step_pairs, n_groups, n_eff, shift, qscale, split_q, combine,
                 post_norm):
    g2 = 2 * n_step_pairs
    q_refs = [refs[2 * pp:2 * pp + 2] for pp in range(n_step_pairs)]
    gq_ref, cos_ref, sin_ref = refs[g2:g2 + 3]
    k_refs = [refs[g2 + 3 + 2 * pp:g2 + 5 + 2 * pp] for pp in range(n_step_pairs)]
    vt_refs = refs[2 * g2 + 3:2 * g2 + 3 + n_step_pairs]
    (coef_ref, gpost_ref, o_ref, s0a_scr, s1a_scr, s0b_scr, s1b_scr, qt_scr,
     res_scr) = refs[2 * g2 + 3 + n_step_pairs:]
    seq = vt_refs[0].shape[3]
    tq = ATTN_TQ
    nq = seq // tq
    s_scrs = ((s0a_scr, s1a_scr), (s0b_scr, s1b_scr))
    lane = _lane_iota((tq, LANES))

    def block_rows(i):
        if isinstance(i, int):
            return pl.ds(i * tq, tq)
        return pl.ds(pl.multiple_of(i * tq, tq), tq)

    def fold8(x, op):
        part = op(x.reshape(FOLD_WAYS, x.shape[0] // (8 * FOLD_WAYS), 8, x.shape[1]), axis=1)
        return op(part, axis=0)

    def tree(parts, op):
        while len(parts) > 1:
            parts = [op(a, b) for a, b in zip(parts[::2], parts[1::2])] + parts[len(parts) & ~1:]
        return parts[0]

    def prep_q(pp, i, u, par):
        rows = block_rows(i)
        q = _group_norm(q_refs[pp][u][rows, :], gq_ref[...], n_groups, n_eff)
        q = _rotate(q, cos_ref[rows, :], sin_ref[rows, :], shift) * qscale
        if split_q:
            q = jnp.where((lane < D_HEAD) == (u == 0), q, 0.0)
        qt_scr[2 * par + u] = q.T.astype(BF16)

    def tick(scores=None, softmax=None):
        acc = None
        maxes = []
        if scores is not None:
            sp, su, spar = scores
            qt = qt_scr[2 * spar + su]
        for c0 in range(0, seq, ATTN_TK):
            ks = slice(c0, c0 + ATTN_TK)
            if scores is not None:
                for k0 in range(c0, c0 + ATTN_TK, ATTN_TKQ[combine]):
                    kq = slice(k0, k0 + ATTN_TKQ[combine])
                    s = jnp.dot(k_refs[sp][su][0, 0, kq, :], qt, preferred_element_type=F32)
                    s_scrs[spar][su][kq, :] = s
                    maxes.append(fold8(s, jnp.max))
            if softmax is not None:
                pp, u, par, m = softmax
                p = jnp.exp2(s_scrs[par][u][ks, :] - m).astype(BF16)
                vrows = (slice(u * VT_HALF, (u + 1) * VT_HALF)
                         if combine == "select" else slice(0, VT_HALF + D_HEAD))
                pv = jnp.dot(vt_refs[pp][0, 0, vrows, ks], p, preferred_element_type=F32)
                acc = pv if acc is None else acc + pv
        m_new = jnp.max(tree(maxes, jnp.maximum), axis=0, keepdims=True) if maxes else None
        res = None
        if acc is not None:
            inv_l = 1.0 / acc[D_HEAD:D_HEAD + 1]
            halves = [acc[h:h + D_HEAD] for h in range(0, acc.shape[0], VT_HALF)]
            res = jnp.concatenate(halves, axis=0) * inv_l
        return m_new, res

    def emit(pp, i, res_e, res_o):
        if combine == "select":
            out_t = jnp.concatenate([res_e, res_o], axis=0)
        else:
            out_t = res_e + coef_ref[0:1, :] * res_o
        if post_norm:
            ms = jnp.mean(out_t * out_t, axis=0, keepdims=True)
            out_t = out_t * lax.rsqrt(ms + RMS_EPS)
        o_ref[block_rows(i), pp * LANES:(pp + 1) * LANES] = (
            out_t.T * gpost_ref[...]).astype(o_ref.dtype)

    assert n_step_pairs == 1 and nq % 2 == 0
    pp = 0

    def advance(i, par, m_o):
        nxt = 1 - par
        m_e_next, res_o = tick(scores=(pp, 0, nxt), softmax=(pp, 1, par, m_o))
        ahead = min(i + 2, nq - 1) if isinstance(i, int) else jnp.minimum(i + 2, nq - 1)
        prep_q(pp, ahead, 0, par)
        emit(pp, i, res_scr[par], res_o)
        m_o_next, res_e = tick(scores=(pp, 1, nxt), softmax=(pp, 0, nxt, m_e_next))
        prep_q(pp, ahead, 1, par)
        res_scr[nxt] = res_e
        return m_o_next

    prep_q(pp, 0, 0, 0)
    prep_q(pp, 0, 1, 0)
    m_e, _ = tick(scores=(pp, 0, 0))
    prep_q(pp, 1, 0, 1)
    m_o, res_e = tick(scores=(pp, 1, 0), softmax=(pp, 0, 0, m_e))
    res_scr[0] = res_e
    prep_q(pp, 1, 1, 1)

    def body(k, m_o):
        return advance(2 * k + 1, 1, advance(2 * k, 0, m_o))

    m_o = lax.fori_loop(0, (nq - 1) // 2, body, m_o)
    m_o = advance(nq - 2, 0, m_o)
    _, res_o = tick(softmax=(pp, 1, 1, m_o))
    emit(pp, nq - 1, res_scr[1], res_o)


def _attention(q_src, q_blk, kn, k_idx, vt, v_idx, gq, cos, sin, coef, gpost, *,
               batch, seq, n_pairs, n_groups, n_eff, shift, scale, split_q, combine,
               post_norm):
    tq = ATTN_TQ
    m = batch * seq
    g = ATTN_STEP_PAIRS
    const = lambda b, s: (0, 0)

    q_specs = [pl.BlockSpec((seq, LANES), lambda b, s, pp=pp, w=w: (b, q_blk(s * g + pp)[w]))
               for pp in range(g) for w in range(2)]
    k_specs = [pl.BlockSpec((1, 1, seq, LANES),
                            lambda b, s, pp=pp, w=w: (b, k_idx(s * g + pp)[w], 0, 0))
               for pp in range(g) for w in range(2)]
    v_specs = [pl.BlockSpec((1, 1, 2 * VT_HALF, seq),
                            lambda b, s, pp=pp: (b, v_idx(s * g + pp), 0, 0))
               for pp in range(g)]
    table = pl.BlockSpec((seq, LANES), const, pipeline_mode=pl.Buffered(1))
    kern = functools.partial(_attn_kernel, n_step_pairs=g, n_groups=n_groups, n_eff=n_eff,
                             shift=shift, qscale=scale * math.log2(math.e), split_q=split_q,
                             combine=combine, post_norm=post_norm)
    return pl.pallas_call(
        kern,
        grid=(batch, n_pairs // g),
        in_specs=(q_specs + [pl.BlockSpec((1, LANES), const), table, table] + k_specs
                  + v_specs + [pl.BlockSpec((8, tq), const), pl.BlockSpec((1, LANES), const)]),
        out_specs=pl.BlockSpec((seq, g * LANES), lambda b, s: (b, s)),
        out_shape=jax.ShapeDtypeStruct((m, n_pairs * LANES), BF16),
        scratch_shapes=[pltpu.VMEM((seq, tq), F32)] * 4 + [
            pltpu.VMEM((4, LANES, tq), BF16),
            pltpu.VMEM((2, D_HEAD if combine == "select" else LANES, tq), F32)],
        compiler_params=_params("parallel", "parallel"),
        name="attention",
    )(*([q_src] * (2 * g)), gq, cos, sin, *([kn] * (2 * g)), *([vt] * g), coef, gpost)


def _natten_kernel(q_ref, k_ref, v_ref, bias_ref, o_ref, q_scr, v_scr, *, rows):
    seq = v_ref.shape[0]
    nwin = WIN_R * GRID_W
    for r0 in range(0, seq, PREP_ROWS):
        sl = slice(r0, r0 + PREP_ROWS)
        q = q_ref[0, 0, sl, :]
        lo = _lane_iota(q.shape) < D_HEAD
        zero = jnp.zeros_like(q)
        q_scr[0, sl, :] = jnp.where(lo, q, zero)
        q_scr[1, sl, :] = jnp.where(lo, zero, q)
        v_scr[sl, :LANES] = v_ref[sl, :].astype(BF16)
        v_scr[sl, LANES:] = jnp.ones((PREP_ROWS, LANES), BF16)

    lane_lo = _lane_iota((GRID_W, LANES)) < D_HEAD

    def body(r, carry):
        w0 = jnp.clip(r - WIN_R // 2, 0, rows - WIN_R)
        var = r - w0
        kbase = pl.multiple_of(w0 * GRID_W, GRID_W)
        qbase = pl.multiple_of(r * GRID_W, GRID_W)
        kw = k_ref[0, 0, pl.ds(kbase, nwin), :]
        vw = v_scr[pl.ds(kbase, nwin), :]
        qrows = pl.ds(qbase, GRID_W)
        q2 = jnp.concatenate([q_scr[0, qrows, :], q_scr[1, qrows, :]], axis=0)
        s = lax.dot_general(q2, kw, (((1,), (1,)), ((), ())),
                            preferred_element_type=F32)
        z = s + bias_ref[0, var]
        zmax = jnp.max(z, axis=-1, keepdims=True)
        p = jnp.exp2(z - zmax).astype(BF16)
        acc = jnp.dot(p, vw, preferred_element_type=F32)
        res = acc[:, :LANES] / acc[:, LANES:]
        o_ref[qrows, :] = jnp.where(lane_lo, res[:GRID_W], res[GRID_W:]).astype(o_ref.dtype)
        return carry

    lax.fori_loop(0, rows, body, 0, unroll=32)


def _natten(tiles, proj, v_blk0, bias, batch, seq):
    rows = seq // GRID_W
    n_pairs = bias.shape[0]
    nwin = WIN_R * GRID_W
    return pl.pallas_call(
        functools.partial(_natten_kernel, rows=rows),
        grid=(batch, n_pairs),
        in_specs=[pl.BlockSpec((1, 1, seq, LANES), lambda b, p: (b, p, 0, 0)),
                  pl.BlockSpec((1, 1, seq, LANES), lambda b, p: (b, n_pairs + p, 0, 0)),
                  pl.BlockSpec((seq, LANES), lambda b, p: (b, v_blk0 + p)),
                  pl.BlockSpec((1, WIN_R, 2 * GRID_W, nwin), lambda b, p: (p, 0, 0, 0))],
        out_specs=pl.BlockSpec((seq, LANES), lambda b, p: (b, p)),
        out_shape=jax.ShapeDtypeStruct((batch * seq, n_pairs * LANES), BF16),
        scratch_shapes=[pltpu.VMEM((2, seq, LANES), BF16),
                        pltpu.VMEM((seq, 2 * LANES), BF16)],
        compiler_params=_params("parallel", "parallel"),
        name="natten",
    )(tiles, tiles, proj, bias)


def _natten_bias(rpb):
    col = np.arange(GRID_W)
    c0 = np.clip(col - WIN_C // 2, 0, GRID_W - WIN_C)
    ci = np.arange(GRID_W)
    inside = (ci[None, :] >= c0[:, None]) & (ci[None, :] < c0[:, None] + WIN_C)
    cb = ci[None, :] - col[:, None] + (WIN_C - 1)
    onehot = (cb[:, :, None] == np.arange(2 * WIN_C - 1)) & inside[:, :, None]
    toep = jnp.einsum("hrb,cjb->hrcj", rpb.astype(F32), jnp.asarray(onehot, F32),
                      precision=lax.Precision.HIGHEST)
    toep = jnp.where(inside[None, None], toep * math.log2(math.e), NEG_BIAS)
    n_heads = rpb.shape[0]
    variants = []
    for v in range(WIN_R):
        rows = toep[:, WIN_R - 1 - v:2 * WIN_R - 1 - v]
        variants.append(rows.transpose(0, 2, 1, 3).reshape(n_heads // 2, 2 * GRID_W,
                                                           WIN_R * GRID_W))
    return jnp.stack(variants, axis=1)


def _rot_tables(parts):
    ang = 0.0
    sign, off = [], 0
    for part in parts:
        if isinstance(part, int):
            sign.append(np.zeros(part, np.float32))
            off += part
            continue
        pos, dim, theta = part
        inv = jnp.power(theta, -jnp.arange(0, dim, 2, dtype=F32) / dim)
        inv_lanes = jnp.pad(jnp.concatenate([inv, inv]), (off, LANES - off - dim))
        ang = ang + pos[:, None] * inv_lanes[None, :]
        sign.append(np.repeat(np.float32([-1.0, 1.0]), dim // 2))
        off += dim
    return jnp.cos(ang), jnp.sin(ang) * jnp.asarray(np.concatenate(sign))[None, :]


def _dup(v):
    return jnp.concatenate([v, v]).reshape(1, -1).astype(F32)


def _pad_lanes(v):
    return jnp.pad(v.astype(F32), (0, LANES - v.shape[0])).reshape(1, LANES)


def kernel(x, norm_e, w_in_e, gq_a, gk_a, rpb_a, gq_b, gk_b, w_out_e, norm_o, w_in_o, g_cq, w_cq_b, g_ckv, w_ckv_b, gq_c, gk_c, gq_d, gk_d, lam_q1, lam_k1, lam_q2, lam_k2, g_sub_d, w_out_o):
    batch, seq, d = x.shape
    m = batch * seq
    x2 = x.reshape(m, d)
    t = jnp.arange(seq)
    pos = t.astype(F32)
    row = (t // GRID_W).astype(F32)
    col = (t % GRID_W).astype(F32)
    half = D_HEAD // 2
    ones_row = jnp.ones((1, LANES), F32)
    unit_coef = jnp.ones((8, ATTN_TQ), F32)

    n_a = 8 * D_HEAD
    base_b = 3 * n_a
    kb0 = base_b + 8 * D_HEAD
    vb0 = kb0 + 2 * D_HEAD
    gate0 = vb0 + 2 * D_HEAD
    w0 = w_in_e[0]
    dup_heads = [w0[:, c:c + D_HEAD] for c in (kb0, kb0, kb0 + D_HEAD, kb0 + D_HEAD,
                                                 vb0, vb0, vb0 + D_HEAD, vb0 + D_HEAD)]
    w_e = jnp.concatenate([w0[:, :kb0]] + dup_heads + [w0[:, gate0:]],
                          axis=1).astype(BF16)
    axial = [(row, half, AXIAL_THETA), (col, half, AXIAL_THETA)]
    cos_b, sin_b = _rot_tables(axial + axial)
    gains_e = jnp.concatenate([_dup(gq_a[0]) * (D_HEAD ** -0.5 * math.log2(math.e)),
                               _dup(gk_a[0]), _dup(gk_b[0])], axis=0)
    tiles_e = (tuple((t, 0, False) for t in range(0, 4))
               + tuple((t, 1, False) for t in range(4, 8)) + ((16, 2, True), (17, 2, True)))
    proj_e, gate_e, kt_e, vp_b = _proj_in(
        x2, norm_e[0], w_e, gains_e, cos_b, sin_b, batch=batch, seq=seq, n_gate=d,
        k_tiles=tiles_e, v_tiles=(18, 19), n_groups=2, n_eff=D_HEAD, shift=half // 2)
    bias = _natten_bias(rpb_a[0])
    oa = _natten(kt_e, proj_e, 8, bias, batch, seq)
    ob = _attention(proj_e, lambda p: (12 + p, 12 + p), kt_e,
                    lambda p: (8 + p // 2, 8 + p // 2),
                    vp_b, lambda p: p // 2, _dup(gq_b[0]), cos_b, sin_b, unit_coef, ones_row,
                    batch=batch, seq=seq, n_pairs=4, n_groups=2, n_eff=D_HEAD,
                    shift=half // 2, scale=D_HEAD ** -0.5, split_q=True, combine="select",
                    post_norm=False)
    x2 = _proj_out(oa, ob, gate_e, w_out_e[0].astype(BF16), x2)

    w1 = w_in_o[0]
    q_lora, kv_lora = g_cq.shape[1], g_ckv.shape[1]
    o_kpe = q_lora + kv_lora
    o_qd = o_kpe + C_ROPE
    zeros = functools.partial(jnp.zeros, dtype=w1.dtype)
    w_o = jnp.concatenate([
        w1[:, :o_kpe],
        zeros((d, C_NOPE)), w1[:, o_kpe:o_qd], zeros((d, LANES - C_NOPE - C_ROPE)),
        w1[:, o_qd:]], axis=1).astype(BF16)
    partial = [(pos, ROT_DIM, ROPE_THETA), D_HEAD - ROT_DIM]
    cos_d, sin_d = _rot_tables(partial + partial)
    n_hc = 8
    dqc = C_NOPE + C_ROPE
    wq = jnp.pad(w_cq_b[0].reshape(q_lora, n_hc, dqc),
                 ((0, 0), (0, 0), (0, LANES - dqc))).reshape(q_lora, n_hc * LANES).astype(BF16)
    wkv = w_ckv_b[0].reshape(kv_lora, n_hc, C_NOPE + C_V)
    wk = jnp.pad(wkv[:, :, :C_NOPE],
                 ((0, 0), (0, 0), (0, LANES - C_NOPE))).reshape(kv_lora, n_hc * LANES).astype(BF16)
    wv = wkv[:, :, C_NOPE:].reshape(kv_lora, n_hc * C_V).astype(BF16)
    cos_c, sin_c = _rot_tables([C_NOPE, (pos, C_ROPE, MLA_THETA), LANES - C_NOPE - C_ROPE])
    proj_o, gate_o, kt_d, vp_d, q_c, kt_c, vp_c = _proj_in(
        x2, norm_o[0], w_o, _dup(gk_d[0]), cos_d, sin_d, batch=batch, seq=seq, n_gate=d,
        k_tiles=tuple((t, 0, True) for t in range(8, 12)), v_tiles=(12, 13, 14, 15),
        n_groups=2, n_eff=D_HEAD, shift=ROT_DIM // 2,
        latent=(g_cq[0], g_ckv[0], wq, wk, wv, _pad_lanes(gk_c[0]), cos_c, sin_c, dqc,
                C_ROPE // 2))
    oc = _attention(q_c, lambda p: (2 * p, 2 * p + 1), kt_c, lambda p: (2 * p, 2 * p + 1),
                    vp_c, lambda p: p, _pad_lanes(gq_c[0]), cos_c, sin_c, unit_coef, ones_row,
                    batch=batch, seq=seq, n_pairs=4, n_groups=1, n_eff=dqc,
                    shift=C_ROPE // 2, scale=dqc ** -0.5, split_q=False, combine="select",
                    post_norm=False)

    lam_init = 0.8 - 0.6 * math.exp(-0.3 * 1)
    lam = (jnp.exp(jnp.sum(lam_q1[0].astype(F32) * lam_k1[0].astype(F32)))
           - jnp.exp(jnp.sum(lam_q2[0].astype(F32) * lam_k2[0].astype(F32))) + lam_init)
    diff_coef = jnp.full((8, ATTN_TQ), -lam, F32)
    gpost = (g_sub_d[0].astype(F32) * (1.0 - lam_init)).reshape(1, LANES)
    od = _attention(proj_o, lambda p: (4 + p, 4 + p), kt_d, lambda p: (p, p),
                    vp_d, lambda p: p, _dup(gq_d[0]), cos_d, sin_d, diff_coef, gpost,
                    batch=batch, seq=seq, n_pairs=4, n_groups=2, n_eff=D_HEAD,
                    shift=ROT_DIM // 2, scale=D_HEAD ** -0.5, split_q=True, combine="add",
                    post_norm=True)
    x2 = _proj_out(oc, od, gate_o, w_out_o[0].astype(BF16), x2)
    return x2.reshape(batch, seq, d)
```
